```python
import jax, jax.numpy as jnp
from jax import lax
import numpy as np

D_MODEL = 1024
BATCH = 4
SEQ = 8192
DEPTH = 4
DEC_BATCH = 8
DEC_SEQ = 64
PAST_LEN = 2048

CHUNK = 64
N_MIXERS = 2
N_ATTN = (DEPTH + 1) // 2
N_HGRN = DEPTH // 2
N_HEADS = 16
N_KV = 4
HEAD_DIM = 64
GROUP = N_HEADS // N_KV
ROT_DIM = HEAD_DIM // 4
ROPE_THETA = 500000.0
WINDOW = 128
WIN_CHUNKS = WINDOW // CHUNK
Q_DIM = N_HEADS * HEAD_DIM
KV_DIM = N_KV * HEAD_DIM
HG_EXPAND = 128
HG_HEADS = D_MODEL // HG_EXPAND
HG_DK = HG_EXPAND
HG_DV = D_MODEL // HG_HEADS
HG_F = HG_HEADS * HG_DK
D_FF = 4 * D_MODEL
EPS = 1e-5

kernel_name = "hybrid_swa_sink_hgrn2_stream_step"


def rmsnorm(x, g):
    xf = x.astype(jnp.float32)
    y = xf * lax.rsqrt(jnp.mean(xf * xf, axis=-1, keepdims=True) + EPS)
    return (y * g.astype(jnp.float32)).astype(x.dtype)


def rope_partial(x, pos):
    half = ROT_DIM // 2
    inv_freq = ROPE_THETA ** (-(jnp.arange(half, dtype=jnp.float32) * 2.0) / ROT_DIM)
    ang = pos[:, None] * inv_freq[None, :]
    cos = jnp.cos(ang)[None, :, None, :]
    sin = jnp.sin(ang)[None, :, None, :]
    xf = x.astype(jnp.float32)
    x1, x2, rest = xf[..., :half], xf[..., half:ROT_DIM], xf[..., ROT_DIM:]
    out = jnp.concatenate([x1 * cos - x2 * sin, x2 * cos + x1 * sin, rest], axis=-1)
    return out.astype(x.dtype)


def attn_project(h, w_qkv, pos):
    B, T, _ = h.shape
    qkv = h @ w_qkv
    q, k, v = jnp.split(qkv, [Q_DIM, Q_DIM + KV_DIM], axis=-1)
    q = rope_partial(q.reshape(B, T, N_HEADS, HEAD_DIM), pos)
    k = rope_partial(k.reshape(B, T, N_KV, HEAD_DIM), pos)
    v = v.reshape(B, T, N_KV, HEAD_DIM)
    return q, k, v


def banded_sink_attention(q, k, v, valid, sinks):
    B, N, Q = q.shape[:3]
    qg = q.reshape(B, N, Q, N_KV, GROUP, HEAD_DIM)
    s = jnp.einsum('bnqkgd,bnskd->bnkgqs', qg, k,
                   preferred_element_type=jnp.float32) * (HEAD_DIM ** -0.5)
    s = jnp.where(valid[None, :, None, None, None, :], s, -jnp.inf)
    sink = sinks.astype(jnp.float32).reshape(N_KV, GROUP)[None, None, :, :, None, None]
    m = jnp.maximum(jnp.max(s, axis=-1, keepdims=True), sink)
    e = jnp.exp(s - m)
    p = e / (jnp.sum(e, axis=-1, keepdims=True) + jnp.exp(sink - m))
    o = jnp.einsum('bnkgqs,bnskd->bnqkgd', p.astype(v.dtype), v)
    return o.reshape(B, N, Q, N_HEADS, HEAD_DIM)


def swa_prompt(h, w_qkv, w_o, sinks):
    B, T, _ = h.shape
    pos = jnp.arange(T, dtype=jnp.float32)
    q, k, v = attn_project(h, w_qkv, pos)
    nc = T // CHUNK
    qc = q.reshape(B, nc, CHUNK, N_HEADS, HEAD_DIM)
    pad = ((0, 0), (WIN_CHUNKS, 0), (0, 0), (0, 0), (0, 0))
    kp = jnp.pad(k.reshape(B, nc, CHUNK, N_KV, HEAD_DIM), pad)
    vp = jnp.pad(v.reshape(B, nc, CHUNK, N_KV, HEAD_DIM), pad)
    kb = jnp.concatenate([kp[:, w:w + nc] for w in range(WIN_CHUNKS + 1)], axis=2)
    vb = jnp.concatenate([vp[:, w:w + nc] for w in range(WIN_CHUNKS + 1)], axis=2)
    key_chunk = jnp.repeat(jnp.arange(WIN_CHUNKS + 1), CHUNK)
    valid = (jnp.arange(nc)[:, None] + key_chunk[None, :] - WIN_CHUNKS) >= 0
    o = banded_sink_attention(qc, kb, vb, valid, sinks)
    out = o.reshape(B, T, Q_DIM) @ w_o
    return out, k[:, T - WINDOW:], v[:, T - WINDOW:]


def swa_sample(h, cache_k, cache_v, w_qkv, w_o, sinks):
    B, T, _ = h.shape
    pos = PAST_LEN + jnp.arange(T, dtype=jnp.float32)
    q, k, v = attn_project(h, w_qkv, pos)
    keys = jnp.concatenate([cache_k.astype(k.dtype), k], axis=1)
    vals = jnp.concatenate([cache_v.astype(v.dtype), v], axis=1)
    valid = jnp.ones((1, keys.shape[1]), dtype=bool)
    o = banded_sink_attention(q[:, None], keys[:, None], vals[:, None], valid, sinks)
    out = o.reshape(B, T, Q_DIM) @ w_o
    W = cache_k.shape[1]
    return out, keys[:, T:].astype(cache_k.dtype), vals[:, T:].astype(cache_v.dtype)


def hgrn_scan(q, k, v, logf, s0):
    B, T, H, DK = q.shape
    DV = v.shape[-1]
    L = CHUNK if T % CHUNK == 0 else T
    n = T // L

    def to_blocks(a):
        return a.reshape(B, n, L, H, a.shape[-1]).transpose(1, 0, 3, 2, 4)

    causal = jnp.tril(jnp.ones((L, L), dtype=bool))[:, :, None]

    def step(S, blk):
        qb, kb, vb, gb = blk
        b = jnp.cumsum(gb, axis=2)
        diff = b[:, :, :, None, :] - b[:, :, None, :, :]
        decay = jnp.where(causal, jnp.exp(jnp.where(causal, diff, 0.0)), 0.0)
        scores = jnp.einsum('bhtk,bhsk,bhtsk->bhts', qb, kb, decay)
        o = (jnp.einsum('bhts,bhsv->bhtv', scores, vb)
             + jnp.einsum('bhtk,bhkv->bhtv', qb * jnp.exp(b), S))
        b_last = b[:, :, -1:, :]
        S_new = (jnp.exp(b_last[:, :, 0, :])[..., None] * S
                 + jnp.einsum('bhsk,bhsv->bhkv', kb * jnp.exp(b_last - b), vb))
        return S_new, o

    S_fin, o = lax.scan(step, s0, (to_blocks(q), to_blocks(k), to_blocks(v), to_blocks(logf)))
    o = o.transpose(1, 0, 3, 2, 4).reshape(B, T, H, DV)
    return o, S_fin


def hgrn2_mix(h, s0, w_in, lb, out_norm, w_o):
    B, T, _ = h.shape
    z = h @ w_in
    zq, zf, zi, zg = jnp.split(z, [HG_F, 2 * HG_F, 2 * HG_F + D_MODEL], axis=-1)
    q = jax.nn.silu(zq.astype(jnp.float32)).reshape(B, T, HG_HEADS, HG_DK)
    zf = zf.astype(jnp.float32).reshape(B, T, HG_HEADS, HG_DK)
    lbh = lb.reshape(HG_HEADS, HG_DK)
    logf = jnp.log(lbh + (1.0 - lbh) * jax.nn.sigmoid(zf))
    k = (1.0 - lbh) * jax.nn.sigmoid(-zf)
    v = zi.astype(jnp.float32).reshape(B, T, HG_HEADS, HG_DV)
    o, S = hgrn_scan(q, k, v, logf, s0)
    o = rmsnorm(o, out_norm) * jax.nn.silu(zg.astype(jnp.float32).reshape(B, T, HG_HEADS, HG_DV))
    out = o.reshape(B, T, D_MODEL).astype(h.dtype) @ w_o
    return out, S


def sqrelu_mlp(h, w_up, w_down):
    return jnp.square(jax.nn.relu(h @ w_up)) @ w_down


def setup_inputs(seed: int = 0) -> dict:
    key = jax.random.key(seed)
    ks = jax.random.split(key, 20)
    f32 = jnp.float32

    def nrm(k, shape, scale):
        return jax.random.normal(k, shape, f32) * scale

    cache_rows = min(WINDOW, PAST_LEN)
    return {
        "x_prompt": nrm(ks[0], (BATCH, SEQ, D_MODEL), 1.0),
        "x_sample": nrm(ks[1], (DEC_BATCH, DEC_SEQ, D_MODEL), 1.0),
        "cache_k": nrm(ks[2], (N_ATTN, DEC_BATCH, cache_rows, N_KV, HEAD_DIM), 1.0),
        "cache_v": nrm(ks[3], (N_ATTN, DEC_BATCH, cache_rows, N_KV, HEAD_DIM), 1.0),
        "state_s": nrm(ks[4], (N_HGRN, DEC_BATCH, HG_HEADS, HG_DK, HG_DV), 0.5),
        "mixer_norm": 1.0 + nrm(ks[5], (DEPTH, D_MODEL), 0.01),
        "mlp_norm": 1.0 + nrm(ks[6], (DEPTH, D_MODEL), 0.01),
        "attn_w_qkv": nrm(ks[7], (N_ATTN, D_MODEL, Q_DIM + 2 * KV_DIM), D_MODEL ** -0.5),
        "attn_w_o": nrm(ks[8], (N_ATTN, Q_DIM, D_MODEL), Q_DIM ** -0.5),
        "attn_sinks": nrm(ks[9], (N_ATTN, N_HEADS), 0.5),
        "hgrn_w_in": nrm(ks[10], (N_HGRN, D_MODEL, 2 * HG_F + 2 * D_MODEL), D_MODEL ** -0.5),
        "hgrn_lb": nrm(ks[11], (N_HGRN, HG_F), 1.0),
        "hgrn_out_norm": 1.0 + nrm(ks[12], (N_HGRN, HG_DV), 0.01),
        "hgrn_w_o": nrm(ks[13], (N_HGRN, D_MODEL, D_MODEL), D_MODEL ** -0.5),
        "mlp_w_up": nrm(ks[14], (DEPTH, D_MODEL, D_FF), D_MODEL ** -0.5),
        "mlp_w_down": nrm(ks[15], (DEPTH, D_FF, D_MODEL), D_FF ** -0.5),
        "final_norm": 1.0 + nrm(ks[16], (D_MODEL,), 0.01),
    }


def reference(x_prompt, x_sample, cache_k, cache_v, state_s, mixer_norm, mlp_norm,
              attn_w_qkv, attn_w_o, attn_sinks, hgrn_w_in, hgrn_lb, hgrn_out_norm,
              hgrn_w_o, mlp_w_up, mlp_w_down, final_norm):
    yp, ys = x_prompt, x_sample
    lb_all = jnp.cumsum(jax.nn.softmax(hgrn_lb.astype(jnp.float32), axis=0), axis=0)
    lb_all = lb_all - lb_all[0:1]
    kp_l, vp_l, sp_l, ks_l, vs_l, ss_l = [], [], [], [], [], []
    for i in range(DEPTH):
        j = i // N_MIXERS
        hp = rmsnorm(yp, mixer_norm[i])
        hs = rmsnorm(ys, mixer_norm[i])
        if i % N_MIXERS == 0:
            op, kp, vp = swa_prompt(hp, attn_w_qkv[j], attn_w_o[j], attn_sinks[j])
            os_, kn, vn = swa_sample(hs, cache_k[j], cache_v[j], attn_w_qkv[j], attn_w_o[j], attn_sinks[j])
            kp_l.append(kp.astype(cache_k.dtype))
            vp_l.append(vp.astype(cache_v.dtype))
            ks_l.append(kn)
            vs_l.append(vn)
        else:
            s0 = jnp.zeros((yp.shape[0], HG_HEADS, HG_DK, HG_DV), jnp.float32)
            op, sp = hgrn2_mix(hp, s0, hgrn_w_in[j], lb_all[j], hgrn_out_norm[j], hgrn_w_o[j])
            os_, sn = hgrn2_mix(hs, state_s[j].astype(jnp.float32), hgrn_w_in[j], lb_all[j],
                                hgrn_out_norm[j], hgrn_w_o[j])
            sp_l.append(sp.astype(state_s.dtype))
            ss_l.append(sn.astype(state_s.dtype))
        yp = yp + op
        ys = ys + os_
        yp = yp + sqrelu_mlp(rmsnorm(yp, mlp_norm[i]), mlp_w_up[i], mlp_w_down[i])
        ys = ys + sqrelu_mlp(rmsnorm(ys, mlp_norm[i]), mlp_w_up[i], mlp_w_down[i])
    yp = rmsnorm(yp, final_norm)
    ys = rmsnorm(ys, final_norm)
    return (yp, ys, jnp.stack(kp_l), jnp.stack(vp_l), jnp.stack(sp_l),
            jnp.stack(ks_l), jnp.stack(vs_l), jnp.stack(ss_l))
```

```python
import functools

import jax
import jax.numpy as jnp
import numpy as np
from jax import lax
from jax.experimental import pallas as pl
from jax.experimental.pallas import tpu as pltpu

D_MODEL = 1024
DEPTH = 4
PAST_LEN = 2048
CHUNK = 64
N_HEADS = 16
N_KV = 4
HEAD_DIM = 64
GROUP = N_HEADS // N_KV
ROT_DIM = HEAD_DIM // 4
ROPE_THETA = 500000.0
WINDOW = 128
Q_DIM = N_HEADS * HEAD_DIM
KV_DIM = N_KV * HEAD_DIM
HG_EXPAND = 128
HG_HEADS = D_MODEL // HG_EXPAND
HG_F = HG_HEADS * HG_EXPAND
D_FF = 4 * D_MODEL
EPS = 1e-5

LANES = 128
VMEM_LIMIT_BYTES = 56 * 1024 * 1024

KEYS = WINDOW + CHUNK
N_LEVELS = 6
assert 1 << N_LEVELS == CHUNK
N_SEG = N_LEVELS + 2

F32 = jnp.float32
BF16 = jnp.bfloat16


def _rms(x, g):
    ms = jnp.mean(x * x, axis=-1, keepdims=True)
    return x * lax.rsqrt(ms + EPS) * g


def _dot(a, b):
    return jnp.dot(a, b, preferred_element_type=F32)


def _dot_nt(a, b):
    return lax.dot_general(a, b, (((1,), (1,)), ((), ())), preferred_element_type=F32)


def _dot_tn(a, b):
    return lax.dot_general(a, b, (((0,), (0,)), ((), ())), preferred_element_type=F32)


def _resident(shape):
    nd = len(shape)
    return pl.BlockSpec(shape, lambda *_: (0,) * nd, pipeline_mode=pl.Buffered(1))


def _row_block(total, want):
    blk = min(total, want)
    while total % blk:
        blk -= CHUNK
    return blk


def _mlp_kernel(x_ref, g_ref, wu_ref, wd_ref, fg_ref, o_ref, *, ff_block, final):
    x = x_ref[...]
    h = _rms(x, g_ref[...]).astype(BF16)
    acc = x
    for j in range(D_FF // ff_block):
        u = _dot(h, wu_ref[:, j * ff_block:(j + 1) * ff_block])
        a = jnp.square(jnp.maximum(u, 0.0)).astype(BF16)
        acc = acc + _dot(a, wd_ref[j * ff_block:(j + 1) * ff_block, :])
    if final:
        acc = _rms(acc, fg_ref[...])
    o_ref[...] = acc


def _mlp(x, g, w_up, w_down, final_g, final):
    m = x.shape[0]
    tm = _row_block(m, 512)
    return pl.pallas_call(
        functools.partial(_mlp_kernel, ff_block=1024, final=final),
        grid=(m // tm,),
        in_specs=[
            pl.BlockSpec((tm, D_MODEL), lambda i: (i, 0)),
            _resident((1, D_MODEL)),
            _resident((D_MODEL, D_FF)),
            _resident((D_FF, D_MODEL)),
            _resident((1, D_MODEL)),
        ],
        out_specs=pl.BlockSpec((tm, D_MODEL), lambda i: (i, 0)),
        out_shape=jax.ShapeDtypeStruct((m, D_MODEL), F32),
        compiler_params=pltpu.CompilerParams(
            dimension_semantics=("arbitrary",), vmem_limit_bytes=VMEM_LIMIT_BYTES),
        name="mlp",
    )(x, g, w_up, w_down, final_g)


def _attn_kernel(sink_ref, x_ref, g_ref, wqkv_ref, wo_ref, cos_ref, sa_ref, sb_ref,
                 ck0_ref, cv0_ref, y_ref, nk_ref, nv_ref,
                 q_scr, ka_scr, kb_scr, va_scr, vb_scr, o_scr, *, nb, tq, pos0):
    t = pl.program_id(1)
    rows = nb * tq
    nch = tq // CHUNK

    @pl.when(t == 0)
    def _():
        nk_ref[...] = ck0_ref[...]
        nv_ref[...] = cv0_ref[...]

    x = x_ref[...].reshape(rows, D_MODEL)
    h = _rms(x, g_ref[...]).astype(BF16)
    qkv = _dot(h, wqkv_ref[...])

    cos, sa, sb = cos_ref[...], sa_ref[...], sb_ref[...]

    def rope(xs):
        return (xs * cos + pltpu.roll(xs, ROT_DIM // 2, 1) * sa
                + pltpu.roll(xs, LANES - ROT_DIM // 2, 1) * sb)

    low = lax.broadcasted_iota(jnp.int32, (1, LANES), 1) < HEAD_DIM
    scale = HEAD_DIM ** -0.5

    for i in range(nb):
        rs = slice(i * tq, (i + 1) * tq)
        for s in range(Q_DIM // LANES):
            qs = rope(qkv[rs, s * LANES:(s + 1) * LANES]) * scale
            q_scr[rs, s * LANES:(s + 1) * LANES] = qs.astype(BF16)
        for s in range(KV_DIM // LANES):
            c0 = Q_DIM + s * LANES
            knew = rope(qkv[rs, c0:c0 + LANES])
            vnew = qkv[rs, c0 + KV_DIM:c0 + KV_DIM + LANES]
            kall = jnp.concatenate([nk_ref[i, :, s * LANES:(s + 1) * LANES], knew], axis=0)
            vall = jnp.concatenate([nv_ref[i, :, s * LANES:(s + 1) * LANES], vnew], axis=0)
            nk_ref[i, :, s * LANES:(s + 1) * LANES] = kall[tq:tq + WINDOW]
            nv_ref[i, :, s * LANES:(s + 1) * LANES] = vall[tq:tq + WINDOW]
            krot = pltpu.roll(kall, HEAD_DIM, 1)
            vrot = pltpu.roll(vall, HEAD_DIM, 1)
            zero = jnp.zeros_like(kall)
            ka_scr[i, 2 * s] = jnp.where(low, kall, zero).astype(BF16)
            kb_scr[i, 2 * s] = jnp.where(low, zero, krot).astype(BF16)
            ka_scr[i, 2 * s + 1] = jnp.where(low, krot, zero).astype(BF16)
            kb_scr[i, 2 * s + 1] = jnp.where(low, zero, kall).astype(BF16)
            va_scr[i, 2 * s] = jnp.where(low, vall, zero).astype(BF16)
            vb_scr[i, 2 * s] = jnp.where(low, zero, vrot).astype(BF16)
            va_scr[i, 2 * s + 1] = jnp.where(low, vrot, zero).astype(BF16)
            vb_scr[i, 2 * s + 1] = jnp.where(low, zero, vall).astype(BF16)

    key_iota = lax.broadcasted_iota(jnp.int32, (1, KEYS), 1)

    def softmax_sink(s, valid, sink):
        s = jnp.where(valid, s, -jnp.inf)
        m = jnp.maximum(jnp.max(s, axis=-1, keepdims=True), sink)
        e = jnp.exp(s - m)
        den = jnp.sum(e, axis=-1, keepdims=True) + jnp.exp(sink - m)
        return (e / den).astype(BF16)

    def chunk_body(idx, carry):
        i = idx // nch
        c = idx % nch
        r0 = pl.multiple_of(idx * CHUNK, CHUNK)
        k0 = pl.multiple_of(c * CHUNK, CHUNK)
        valid = (pos0 + t * tq + c * CHUNK - WINDOW + key_iota) >= 0
        for hh in range(N_KV):
            ka = ka_scr[i, hh, pl.ds(k0, KEYS), :]
            kb = kb_scr[i, hh, pl.ds(k0, KEYS), :]
            va = va_scr[i, hh, pl.ds(k0, KEYS), :]
            vb = vb_scr[i, hh, pl.ds(k0, KEYS), :]
            for j in range(GROUP // 2):
                slab = hh * (GROUP // 2) + j
                qs = q_scr[pl.ds(r0, CHUNK), slab * LANES:(slab + 1) * LANES]
                pa = softmax_sink(_dot_nt(qs, ka), valid, sink_ref[2 * slab])
                pb = softmax_sink(_dot_nt(qs, kb), valid, sink_ref[2 * slab + 1])
                o = _dot(pa, va) + _dot(pb, vb)
                o_scr[pl.ds(r0, CHUNK), slab * LANES:(slab + 1) * LANES] = o.astype(BF16)
        return carry

    lax.fori_loop(0, nb * nch, chunk_body, 0)
    y = x + _dot(o_scr[...], wo_ref[...])
    y_ref[...] = y.reshape(nb, tq, D_MODEL)


def _attn_layer(x, g, w_qkv, w_o, sinks, rope_tabs, cache_k, cache_v, *, nb, tq, pos0):
    b, t_len, _ = x.shape
    cos, sa, sb = rope_tabs
    kvlen = WINDOW + tq
    cache_blk = pl.BlockSpec((nb, WINDOW, KV_DIM), lambda bi, ti: (bi, 0, 0))
    tab_blk = pl.BlockSpec((tq, LANES), lambda bi, ti: (ti, 0))
    return pl.pallas_call(
        functools.partial(_attn_kernel, nb=nb, tq=tq, pos0=pos0),
        grid=(b // nb, t_len // tq),
        in_specs=[
            pl.BlockSpec(memory_space=pltpu.SMEM),
            pl.BlockSpec((nb, tq, D_MODEL), lambda bi, ti: (bi, ti, 0)),
            _resident((1, D_MODEL)),
            _resident((D_MODEL, Q_DIM + 2 * KV_DIM)),
            _resident((Q_DIM, D_MODEL)),
            tab_blk, tab_blk, tab_blk,
            cache_blk, cache_blk,
        ],
        out_specs=[
            pl.BlockSpec((nb, tq, D_MODEL), lambda bi, ti: (bi, ti, 0)),
            cache_blk, cache_blk,
        ],
        out_shape=[
            jax.ShapeDtypeStruct((b, t_len, D_MODEL), F32),
            jax.ShapeDtypeStruct((b, WINDOW, KV_DIM), F32),
            jax.ShapeDtypeStruct((b, WINDOW, KV_DIM), F32),
        ],
        scratch_shapes=[
            pltpu.VMEM((nb * tq, Q_DIM), BF16),
            pltpu.VMEM((nb, N_KV, kvlen, LANES), BF16),
            pltpu.VMEM((nb, N_KV, kvlen, LANES), BF16),
            pltpu.VMEM((nb, N_KV, kvlen, LANES), BF16),
            pltpu.VMEM((nb, N_KV, kvlen, LANES), BF16),
            pltpu.VMEM((nb * tq, Q_DIM), BF16),
        ],
        compiler_params=pltpu.CompilerParams(
            dimension_semantics=("arbitrary", "arbitrary"), vmem_limit_bytes=VMEM_LIMIT_BYTES),
        name="swa_layer",
    )(sinks, x, g, w_qkv, w_o, cos, sa, sb, cache_k, cache_v)


def _rope_tables(pos):
    half = ROT_DIM // 2
    inv_freq = ROPE_THETA ** (-(jnp.arange(half, dtype=F32) * 2.0) / ROT_DIM)
    ang = pos[:, None] * inv_freq[None, :]
    cos, sin = jnp.cos(ang), jnp.sin(ang)
    n = pos.shape[0]
    rest = HEAD_DIM - ROT_DIM
    c64 = jnp.concatenate([cos, cos, jnp.ones((n, rest), F32)], axis=1)
    sa64 = jnp.concatenate([jnp.zeros((n, half), F32), sin, jnp.zeros((n, rest), F32)], axis=1)
    sb64 = jnp.concatenate([-sin, jnp.zeros((n, half + rest), F32)], axis=1)
    rep = LANES // HEAD_DIM
    return tuple(jnp.tile(a, (1, rep)) for a in (c64, sa64, sb64))


def _segment_matrix():
    seg = np.zeros((N_SEG, CHUNK, CHUNK), np.float32)
    for l in range(N_LEVELS):
        n = 1 << l
        for t in range(CHUNK):
            mid = (t // (2 * n)) * 2 * n + n
            if t >= mid:
                seg[l, t, mid:t + 1] = 1.0
            else:
                seg[l, t, t + 1:mid] = 1.0
    for t in range(CHUNK):
        seg[N_LEVELS, t, :t + 1] = 1.0
        seg[N_LEVELS + 1, t, t + 1:] = 1.0
    seg = seg.reshape(N_SEG * CHUNK, CHUNK)
    return np.concatenate([seg, seg, seg], axis=1)


def _hgrn_kernel(x_ref, g_ref, win_ref, lbraw_ref, onorm_ref, wo_ref, seg_ref, s0_ref,
                 y_ref, st_ref, z_scr, o_scr, *, nb, tq, layer):
    t = pl.program_id(1)
    rows = nb * tq
    nch = tq // CHUNK

    @pl.when(t == 0)
    def _():
        st_ref[...] = s0_ref[...]

    x = x_ref[...].reshape(rows, D_MODEL)
    h = _rms(x, g_ref[...]).astype(BF16)
    z_scr[...] = _dot(h, win_ref[...])

    lbraw = lbraw_ref[...]
    e = jnp.exp(lbraw - jnp.max(lbraw, axis=0, keepdims=True))
    sm = e / jnp.sum(e, axis=0, keepdims=True)
    cs0 = sm[0:1]
    cs = cs0
    for r in range(1, layer + 1):
        cs = cs + sm[r:r + 1]
    lb = cs - cs0
    one_m_lb = 1.0 - lb
    onorm = onorm_ref[...]

    ti = lax.broadcasted_iota(jnp.int32, (CHUNK, CHUNK), 0)
    si = lax.broadcasted_iota(jnp.int32, (CHUNK, CHUNK), 1)
    masks = [ti == si]
    for l in range(N_LEVELS):
        masks.append(((ti >> (l + 1)) == (si >> (l + 1)))
                     & (((ti >> l) & 1) == 1) & (((si >> l) & 1) == 0))

    def chunk_body(idx, carry):
        i = idx // nch
        r0 = pl.multiple_of(idx * CHUNK, CHUNK)
        zq = z_scr[pl.ds(r0, CHUNK), 0:HG_F]
        zf = z_scr[pl.ds(r0, CHUNK), HG_F:2 * HG_F]
        zi = z_scr[pl.ds(r0, CHUNK), 2 * HG_F:2 * HG_F + D_MODEL]
        zg = z_scr[pl.ds(r0, CHUNK), 2 * HG_F + D_MODEL:]
        q = zq * (1.0 / (1.0 + jnp.exp(-zq)))
        et = jnp.exp(-jnp.abs(zf))
        rt = 1.0 / (1.0 + et)
        nonneg = zf >= 0.0
        sig_pos = jnp.where(nonneg, rt, et * rt)
        sig_neg = jnp.where(nonneg, et * rt, rt)
        logf = jnp.log(lb + one_m_lb * sig_pos)
        kk = one_m_lb * sig_neg
        gate = zg * (1.0 / (1.0 + jnp.exp(-zg)))
        hi = logf.astype(BF16)
        r1 = logf - hi.astype(F32)
        mid = r1.astype(BF16)
        lo = (r1 - mid.astype(F32)).astype(BF16)
        dall = _dot(seg_ref[...], jnp.concatenate([hi, mid, lo], axis=0))
        eall = jnp.exp(dall)
        for hh in range(HG_HEADS):
            cs_ = slice(hh * HG_EXPAND, (hh + 1) * HG_EXPAND)
            qh, kh, vh = q[:, cs_], kk[:, cs_], zi[:, cs_]
            vhb = vh.astype(BF16)
            sc = jnp.where(masks[0], _dot_nt(qh.astype(BF16), kh.astype(BF16)), 0.0)
            for l in range(N_LEVELS):
                el = eall[l * CHUNK:(l + 1) * CHUNK, cs_]
                p = _dot_nt((qh * el).astype(BF16), (kh * el).astype(BF16))
                sc = sc + jnp.where(masks[l + 1], p, 0.0)
            eb = eall[N_LEVELS * CHUNK:(N_LEVELS + 1) * CHUNK, cs_]
            er = eall[(N_LEVELS + 1) * CHUNK:(N_LEVELS + 2) * CHUNK, cs_]
            st = st_ref[i, hh]
            o = _dot(sc.astype(BF16), vhb) + _dot((qh * eb).astype(BF16), st.astype(BF16))
            eb_last = jnp.broadcast_to(eb[CHUNK - 1:CHUNK, :], (8, HG_EXPAND))
            eb_col = jnp.transpose(eb_last)[:, 0:1]
            st_ref[i, hh] = eb_col * st + _dot_tn((kh * er).astype(BF16), vhb)
            on = o * lax.rsqrt(jnp.mean(o * o, axis=-1, keepdims=True) + EPS) * onorm
            o_scr[pl.ds(r0, CHUNK), cs_] = (on * gate[:, cs_]).astype(BF16)
        return carry

    lax.fori_loop(0, nb * nch, chunk_body, 0)
    y = x + _dot(o_scr[...], wo_ref[...])
    y_ref[...] = y.reshape(nb, tq, D_MODEL)


def _hgrn_layer(x, g, w_in, lb_raw, out_norm, w_o, seg, s0, *, nb, tq, layer):
    b, t_len, _ = x.shape
    n_hgrn = lb_raw.shape[0]
    st_blk = pl.BlockSpec((nb, HG_HEADS, HG_EXPAND, HG_EXPAND), lambda bi, ti: (bi, 0, 0, 0))
    return pl.pallas_call(
        functools.partial(_hgrn_kernel, nb=nb, tq=tq, layer=layer),
        grid=(b // nb, t_len // tq),
        in_specs=[
            pl.BlockSpec((nb, tq, D_MODEL), lambda bi, ti: (bi, ti, 0)),
            _resident((1, D_MODEL)),
            _resident((D_MODEL, 2 * HG_F + 2 * D_MODEL)),
            _resident((n_hgrn, HG_F)),
            _resident((1, HG_EXPAND)),
            _resident((D_MODEL, D_MODEL)),
            _resident((N_SEG * CHUNK, 3 * CHUNK)),
            st_blk,
        ],
        out_specs=[
            pl.BlockSpec((nb, tq, D_MODEL), lambda bi, ti: (bi, ti, 0)),
            st_blk,
        ],
        out_shape=[
            jax.ShapeDtypeStruct((b, t_len, D_MODEL), F32),
            jax.ShapeDtypeStruct((b, HG_HEADS, HG_EXPAND, HG_EXPAND), F32),
        ],
        scratch_shapes=[
            pltpu.VMEM((nb * tq, 2 * HG_F + 2 * D_MODEL), F32),
            pltpu.VMEM((nb * tq, D_MODEL), BF16),
        ],
        compiler_params=pltpu.CompilerParams(
            dimension_semantics=("arbitrary", "arbitrary"), vmem_limit_bytes=VMEM_LIMIT_BYTES),
        name="hgrn_layer",
    )(x, g, w_in, lb_raw, out_norm, w_o, seg, s0)


def kernel(x_prompt, x_sample, cache_k, cache_v, state_s, mixer_norm, mlp_norm, attn_w_qkv,
           attn_w_o, attn_sinks, hgrn_w_in, hgrn_lb, hgrn_out_norm, hgrn_w_o, mlp_w_up,
           mlp_w_down, final_norm):
    bp, tp, _ = x_prompt.shape
    bs, ts, _ = x_sample.shape
    assert cache_k.shape[2] == WINDOW and ts % CHUNK == 0 and tp % CHUNK == 0

    w_qkv = attn_w_qkv.astype(BF16)
    w_ao = attn_w_o.astype(BF16)
    w_in = hgrn_w_in.astype(BF16)
    w_ho = hgrn_w_o.astype(BF16)
    w_up = mlp_w_up.astype(BF16)
    w_down = mlp_w_down.astype(BF16)
    lb_raw = hgrn_lb.astype(F32)
    seg = jnp.asarray(_segment_matrix(), BF16)
    final_g = final_norm.reshape(1, D_MODEL)

    rope_p = _rope_tables(jnp.arange(tp, dtype=F32))
    rope_s = _rope_tables(PAST_LEN + jnp.arange(ts, dtype=F32))
    tq_p = _row_block(tp, 256)
    zero_cache = jnp.zeros((bp, WINDOW, KV_DIM), F32)
    zero_state = jnp.zeros((bp, HG_HEADS, HG_EXPAND, HG_EXPAND), F32)

    yp, ys = x_prompt, x_sample
    kp_l, vp_l, sp_l, ks_l, vs_l, ss_l = [], [], [], [], [], []
    for i in range(DEPTH):
        j = i // 2
        g = mixer_norm[i].reshape(1, D_MODEL)
        if i % 2 == 0:
            yp, kp, vp = _attn_layer(yp, g, w_qkv[j], w_ao[j], attn_sinks[j], rope_p,
                                     zero_cache, zero_cache, nb=1, tq=tq_p, pos0=0)
            ys, kn, vn = _attn_layer(ys, g, w_qkv[j], w_ao[j], attn_sinks[j], rope_s,
                                     cache_k[j].reshape(bs, WINDOW, KV_DIM),
                                     cache_v[j].reshape(bs, WINDOW, KV_DIM),
                                     nb=1, tq=ts, pos0=PAST_LEN)
            kp_l.append(kp.reshape(bp, WINDOW, N_KV, HEAD_DIM))
            vp_l.append(vp.reshape(bp, WINDOW, N_KV, HEAD_DIM))
            ks_l.append(kn.reshape(bs, WINDOW, N_KV, HEAD_DIM))
            vs_l.append(vn.reshape(bs, WINDOW, N_KV, HEAD_DIM))
        else:
            onorm = hgrn_out_norm[j].reshape(1, HG_EXPAND)
            yp, sp = _hgrn_layer(yp, g, w_in[j], lb_raw, onorm, w_ho[j], seg, zero_state,
                                 nb=1, tq=tq_p, layer=j)
            ys, sn = _hgrn_layer(ys, g, w_in[j], lb_raw, onorm, w_ho[j], seg, state_s[j],
                                 nb=1, tq=ts, layer=j)
            sp_l.append(sp)
            ss_l.append(sn)
        gm = mlp_norm[i].reshape(1, D_MODEL)
        final = i == DEPTH - 1
        yp = _mlp(yp.reshape(bp * tp, D_MODEL), gm, w_up[i], w_down[i], final_g,
                  final).reshape(bp, tp, D_MODEL)
        ys = _mlp(ys.reshape(bs * ts, D_MODEL), gm, w_up[i], w_down[i], final_g,
                  final).reshape(bs, ts, D_MODEL)
    return (yp, ys, jnp.stack(kp_l), jnp.stack(vp_l), jnp.stack(sp_l),
            jnp.stack(ks_l), jnp.stack(vs_l), jnp.stack(ss_l))
```

```python
import functools

import jax
import jax.numpy as jnp
import numpy as np
from jax import lax
from jax.experimental import pallas as pl
from jax.experimental.pallas import tpu as pltpu

D_MODEL = 1024
DEPTH = 4
PAST_LEN = 2048
CHUNK = 64
N_HEADS = 16
N_KV = 4
HEAD_DIM = 64
GROUP = N_HEADS // N_KV
ROT_DIM = HEAD_DIM // 4
ROPE_THETA = 500000.0
WINDOW = 128
Q_DIM = N_HEADS * HEAD_DIM
KV_DIM = N_KV * HEAD_DIM
HG_EXPAND = 128
HG_HEADS = D_MODEL // HG_EXPAND
HG_F = HG_HEADS * HG_EXPAND
D_FF = 4 * D_MODEL
EPS = 1e-5

LANES = 128
VMEM_LIMIT_BYTES = 56 * 1024 * 1024

KEYS = WINDOW + CHUNK
N_LEVELS = 6
assert 1 << N_LEVELS == CHUNK
N_SEG = N_LEVELS + 2

F32 = jnp.float32
BF16 = jnp.bfloat16


def _rms(x, g):
    ms = jnp.mean(x * x, axis=-1, keepdims=True)
    return x * lax.rsqrt(ms + EPS) * g


def _dot(a, b):
    return jnp.dot(a, b, preferred_element_type=F32)


def _dot_nt(a, b):
    return lax.dot_general(a, b, (((1,), (1,)), ((), ())), preferred_element_type=F32)


def _dot_tn(a, b):
    return lax.dot_general(a, b, (((0,), (0,)), ((), ())), preferred_element_type=F32)


def _resident(shape):
    nd = len(shape)
    return pl.BlockSpec(shape, lambda *_: (0,) * nd, pipeline_mode=pl.Buffered(1))


def _row_block(total, want):
    blk = min(total, want)
    while total % blk:
        blk -= CHUNK
    return blk


def _mlp_kernel(x_ref, g_ref, wu_ref, wd_ref, fg_ref, o_ref, *, ff_block, final):
    x = x_ref[...]
    h = _rms(x, g_ref[...]).astype(BF16)
    acc = x
    for j in range(D_FF // ff_block):
        u = _dot(h, wu_ref[:, j * ff_block:(j + 1) * ff_block])
        a = jnp.square(jnp.maximum(u, 0.0)).astype(BF16)
        acc = acc + _dot(a, wd_ref[j * ff_block:(j + 1) * ff_block, :])
    if final:
        acc = _rms(acc, fg_ref[...])
    o_ref[...] = acc


def _mlp(x, g, w_up, w_down, final_g, final):
    m = x.shape[0]
    tm = _row_block(m, 512)
    return pl.pallas_call(
        functools.partial(_mlp_kernel, ff_block=1024, final=final),
        grid=(m // tm,),
        in_specs=[
            pl.BlockSpec((tm, D_MODEL), lambda i: (i, 0)),
            _resident((1, D_MODEL)),
            _resident((D_MODEL, D_FF)),
            _resident((D_FF, D_MODEL)),
            _resident((1, D_MODEL)),
        ],
        out_specs=pl.BlockSpec((tm, D_MODEL), lambda i: (i, 0)),
        out_shape=jax.ShapeDtypeStruct((m, D_MODEL), F32),
        compiler_params=pltpu.CompilerParams(
            dimension_semantics=("arbitrary",), vmem_limit_bytes=VMEM_LIMIT_BYTES),
        name="mlp",
    )(x, g, w_up, w_down, final_g)


def _attn_kernel(sink_ref, x_ref, g_ref, wqkv_ref, wo_ref, cos_ref, sa_ref, sb_ref,
                 ck0_ref, cv0_ref, y_ref, nk_ref, nv_ref,
                 q_scr, ka_scr, kb_scr, va_scr, vb_scr, o_scr, *, nb, tq, pos0):
    t = pl.program_id(1)
    rows = nb * tq
    nch = tq // CHUNK

    @pl.when(t == 0)
    def _():
        nk_ref[...] = ck0_ref[...]
        nv_ref[...] = cv0_ref[...]

    x = x_ref[...].reshape(rows, D_MODEL)
    h = _rms(x, g_ref[...]).astype(BF16)
    qkv = _dot(h, wqkv_ref[...])

    cos, sa, sb = cos_ref[...], sa_ref[...], sb_ref[...]

    def rope(xs):
        return (xs * cos + pltpu.roll(xs, ROT_DIM // 2, 1) * sa
                + pltpu.roll(xs, LANES - ROT_DIM // 2, 1) * sb)

    low = lax.broadcasted_iota(jnp.int32, (1, LANES), 1) < HEAD_DIM
    scale = HEAD_DIM ** -0.5

    for i in range(nb):
        rs = slice(i * tq, (i + 1) * tq)
        for s in range(Q_DIM // LANES):
            qs = rope(qkv[rs, s * LANES:(s + 1) * LANES]) * scale
            q_scr[rs, s * LANES:(s + 1) * LANES] = qs.astype(BF16)
        for s in range(KV_DIM // LANES):
            c0 = Q_DIM + s * LANES
            knew = rope(qkv[rs, c0:c0 + LANES])
            vnew = qkv[rs, c0 + KV_DIM:c0 + KV_DIM + LANES]
            kall = jnp.concatenate([nk_ref[i, :, s * LANES:(s + 1) * LANES], knew], axis=0)
            vall = jnp.concatenate([nv_ref[i, :, s * LANES:(s + 1) * LANES], vnew], axis=0)
            nk_ref[i, :, s * LANES:(s + 1) * LANES] = kall[tq:tq + WINDOW]
            nv_ref[i, :, s * LANES:(s + 1) * LANES] = vall[tq:tq + WINDOW]
            krot = pltpu.roll(kall, HEAD_DIM, 1)
            vrot = pltpu.roll(vall, HEAD_DIM, 1)
            zero = jnp.zeros_like(kall)
            ka_scr[i, 2 * s] = jnp.where(low, kall, zero).astype(BF16)
            kb_scr[i, 2 * s] = jnp.where(low, zero, krot).astype(BF16)
            ka_scr[i, 2 * s + 1] = jnp.where(low, krot, zero).astype(BF16)
            kb_scr[i, 2 * s + 1] = jnp.where(low, zero, kall).astype(BF16)
            va_scr[i, 2 * s] = jnp.where(low, vall, zero).astype(BF16)
            vb_scr[i, 2 * s] = jnp.where(low, zero, vrot).astype(BF16)
            va_scr[i, 2 * s + 1] = jnp.where(low, vrot, zero).astype(BF16)
            vb_scr[i, 2 * s + 1] = jnp.where(low, zero, vall).astype(BF16)

    key_iota = lax.broadcasted_iota(jnp.int32, (1, KEYS), 1)

    def softmax_sink(s, valid, sink):
        s = jnp.where(valid, s, -jnp.inf)
        m = jnp.maximum(jnp.max(s, axis=-1, keepdims=True), sink)
        e = jnp.exp(s - m)
        den = jnp.sum(e, axis=-1, keepdims=True) + jnp.exp(sink - m)
        return (e / den).astype(BF16)

    sinks = [sink_ref[hd] for hd in range(N_HEADS)]
    n_slab = Q_DIM // LANES

    def scores(i, c):
        r0, k0 = i * tq + c * CHUNK, c * CHUNK
        out = []
        for slab in range(n_slab):
            hh = slab // (GROUP // 2)
            qs = q_scr[r0:r0 + CHUNK, slab * LANES:(slab + 1) * LANES]
            out.append((_dot_nt(qs, ka_scr[i, hh, k0:k0 + KEYS, :]),
                        _dot_nt(qs, kb_scr[i, hh, k0:k0 + KEYS, :])))
        return out

    work = [(i, c) for i in range(nb) for c in range(nch)]
    sc_next = scores(*work[0])
    for w, (i, c) in enumerate(work):
        sc = sc_next
        if w + 1 < len(work):
            sc_next = scores(*work[w + 1])
        r0, k0 = i * tq + c * CHUNK, c * CHUNK
        valid = (pos0 + t * tq + k0 - WINDOW + key_iota) >= 0
        probs = [(softmax_sink(sc[s][0], valid, sinks[2 * s]),
                  softmax_sink(sc[s][1], valid, sinks[2 * s + 1])) for s in range(n_slab)]
        for slab in range(n_slab):
            hh = slab // (GROUP // 2)
            o = (_dot(probs[slab][0], va_scr[i, hh, k0:k0 + KEYS, :])
                 + _dot(probs[slab][1], vb_scr[i, hh, k0:k0 + KEYS, :]))
            o_scr[r0:r0 + CHUNK, slab * LANES:(slab + 1) * LANES] = o.astype(BF16)
    y = x + _dot(o_scr[...], wo_ref[...])
    y_ref[...] = y.reshape(nb, tq, D_MODEL)


def _attn_layer(x, g, w_qkv, w_o, sinks, rope_tabs, cache_k, cache_v, *, nb, tq, pos0):
    b, t_len, _ = x.shape
    cos, sa, sb = rope_tabs
    kvlen = WINDOW + tq
    cache_blk = pl.BlockSpec((nb, WINDOW, KV_DIM), lambda bi, ti: (bi, 0, 0))
    tab_blk = pl.BlockSpec((tq, LANES), lambda bi, ti: (ti, 0))
    return pl.pallas_call(
        functools.partial(_attn_kernel, nb=nb, tq=tq, pos0=pos0),
        grid=(b // nb, t_len // tq),
        in_specs=[
            pl.BlockSpec(memory_space=pltpu.SMEM),
            pl.BlockSpec((nb, tq, D_MODEL), lambda bi, ti: (bi, ti, 0)),
            _resident((1, D_MODEL)),
            _resident((D_MODEL, Q_DIM + 2 * KV_DIM)),
            _resident((Q_DIM, D_MODEL)),
            tab_blk, tab_blk, tab_blk,
            cache_blk, cache_blk,
        ],
        out_specs=[
            pl.BlockSpec((nb, tq, D_MODEL), lambda bi, ti: (bi, ti, 0)),
            cache_blk, cache_blk,
        ],
        out_shape=[
            jax.ShapeDtypeStruct((b, t_len, D_MODEL), F32),
            jax.ShapeDtypeStruct((b, WINDOW, KV_DIM), F32),
            jax.ShapeDtypeStruct((b, WINDOW, KV_DIM), F32),
        ],
        scratch_shapes=[
            pltpu.VMEM((nb * tq, Q_DIM), BF16),
            pltpu.VMEM((nb, N_KV, kvlen, LANES), BF16),
            pltpu.VMEM((nb, N_KV, kvlen, LANES), BF16),
            pltpu.VMEM((nb, N_KV, kvlen, LANES), BF16),
            pltpu.VMEM((nb, N_KV, kvlen, LANES), BF16),
            pltpu.VMEM((nb * tq, Q_DIM), BF16),
        ],
        compiler_params=pltpu.CompilerParams(
            dimension_semantics=("arbitrary", "arbitrary"), vmem_limit_bytes=VMEM_LIMIT_BYTES),
        name="swa_layer",
    )(sinks, x, g, w_qkv, w_o, cos, sa, sb, cache_k, cache_v)


def _rope_tables(pos):
    half = ROT_DIM // 2
    inv_freq = ROPE_THETA ** (-(jnp.arange(half, dtype=F32) * 2.0) / ROT_DIM)
    ang = pos[:, None] * inv_freq[None, :]
    cos, sin = jnp.cos(ang), jnp.sin(ang)
    n = pos.shape[0]
    rest = HEAD_DIM - ROT_DIM
    c64 = jnp.concatenate([cos, cos, jnp.ones((n, rest), F32)], axis=1)
    sa64 = jnp.concatenate([jnp.zeros((n, half), F32), sin, jnp.zeros((n, rest), F32)], axis=1)
    sb64 = jnp.concatenate([-sin, jnp.zeros((n, half + rest), F32)], axis=1)
    rep = LANES // HEAD_DIM
    return tuple(jnp.tile(a, (1, rep)) for a in (c64, sa64, sb64))


def _segment_matrix():
    seg = np.zeros((N_SEG, CHUNK, CHUNK), np.float32)
    for l in range(N_LEVELS):
        n = 1 << l
        for t in range(CHUNK):
            mid = (t // (2 * n)) * 2 * n + n
            if t >= mid:
                seg[l, t, mid:t + 1] = 1.0
            else:
                seg[l, t, t + 1:mid] = 1.0
    for t in range(CHUNK):
        seg[N_LEVELS, t, :t + 1] = 1.0
        seg[N_LEVELS + 1, t, t + 1:] = 1.0
    seg = seg.reshape(N_SEG * CHUNK, CHUNK)
    return np.concatenate([seg, seg, seg], axis=1)


def _hgrn_kernel(x_ref, g_ref, win_ref, lbraw_ref, onorm_ref, wo_ref, seg_ref, s0_ref,
                 y_ref, st_ref, z_scr, o_scr, *, nb, tq, layer):
    t = pl.program_id(1)
    rows = nb * tq
    nch = tq // CHUNK

    @pl.when(t == 0)
    def _():
        st_ref[...] = s0_ref[...]

    x = x_ref[...].reshape(rows, D_MODEL)
    h = _rms(x, g_ref[...]).astype(BF16)
    z_scr[...] = _dot(h, win_ref[...])

    lbraw = lbraw_ref[...]
    e = jnp.exp(lbraw - jnp.max(lbraw, axis=0, keepdims=True))
    sm = e / jnp.sum(e, axis=0, keepdims=True)
    cs0 = sm[0:1]
    cs = cs0
    for r in range(1, layer + 1):
        cs = cs + sm[r:r + 1]
    lb = cs - cs0
    one_m_lb = 1.0 - lb
    onorm = onorm_ref[...]

    ti = lax.broadcasted_iota(jnp.int32, (CHUNK, CHUNK), 0)
    si = lax.broadcasted_iota(jnp.int32, (CHUNK, CHUNK), 1)
    masks = [ti == si]
    for l in range(N_LEVELS):
        masks.append(((ti >> (l + 1)) == (si >> (l + 1)))
                     & (((ti >> l) & 1) == 1) & (((si >> l) & 1) == 0))

    def intra(i, c):
        r0 = i * tq + c * CHUNK
        zq = z_scr[r0:r0 + CHUNK, 0:HG_F]
        zf = z_scr[r0:r0 + CHUNK, HG_F:2 * HG_F]
        zi = z_scr[r0:r0 + CHUNK, 2 * HG_F:2 * HG_F + D_MODEL]
        zg = z_scr[r0:r0 + CHUNK, 2 * HG_F + D_MODEL:]
        q = zq * (1.0 / (1.0 + jnp.exp(-zq)))
        et = jnp.exp(-jnp.abs(zf))
        rt = 1.0 / (1.0 + et)
        nonneg = zf >= 0.0
        sig_pos = jnp.where(nonneg, rt, et * rt)
        sig_neg = jnp.where(nonneg, et * rt, rt)
        logf = jnp.log(lb + one_m_lb * sig_pos)
        kk = one_m_lb * sig_neg
        gate = zg * (1.0 / (1.0 + jnp.exp(-zg)))
        hi = logf.astype(BF16)
        r1 = logf - hi.astype(F32)
        mid = r1.astype(BF16)
        lo = (r1 - mid.astype(F32)).astype(BF16)
        dall = _dot(seg_ref[...], jnp.concatenate([hi, mid, lo], axis=0))
        eall = jnp.exp(dall)
        heads = []
        for hh in range(HG_HEADS):
            cs_ = slice(hh * HG_EXPAND, (hh + 1) * HG_EXPAND)
            qh, kh = q[:, cs_], kk[:, cs_]
            sc = jnp.where(masks[0], _dot_nt(qh.astype(BF16), kh.astype(BF16)), 0.0)
            for l in range(N_LEVELS):
                el = eall[l * CHUNK:(l + 1) * CHUNK, cs_]
                p = _dot_nt((qh * el).astype(BF16), (kh * el).astype(BF16))
                sc = sc + jnp.where(masks[l + 1], p, 0.0)
            eb = eall[N_LEVELS * CHUNK:(N_LEVELS + 1) * CHUNK, cs_]
            er = eall[(N_LEVELS + 1) * CHUNK:(N_LEVELS + 2) * CHUNK, cs_]
            eb_last = jnp.broadcast_to(eb[CHUNK - 1:CHUNK, :], (8, HG_EXPAND))
            heads.append(dict(
                sc=sc.astype(BF16), qe=(qh * eb).astype(BF16), ke=(kh * er).astype(BF16),
                v=zi[:, cs_].astype(BF16), gate=gate[:, cs_],
                eb_col=jnp.transpose(eb_last)[:, 0:1]))
        return heads

    def inter(i, c, heads):
        r0 = i * tq + c * CHUNK
        states = [st_ref[i, hh] for hh in range(HG_HEADS)]
        o_list = [_dot(hd["sc"], hd["v"]) + _dot(hd["qe"], st.astype(BF16))
                  for hd, st in zip(heads, states)]
        new_states = [hd["eb_col"] * st + _dot_tn(hd["ke"], hd["v"])
                      for hd, st in zip(heads, states)]
        for hh in range(HG_HEADS):
            o = o_list[hh]
            on = o * lax.rsqrt(jnp.mean(o * o, axis=-1, keepdims=True) + EPS) * onorm
            st_ref[i, hh] = new_states[hh]
            o_scr[r0:r0 + CHUNK, hh * HG_EXPAND:(hh + 1) * HG_EXPAND] = (
                on * heads[hh]["gate"]).astype(BF16)

    work = [(i, c) for i in range(nb) for c in range(nch)]
    pre_next = intra(*work[0])
    for w, (i, c) in enumerate(work):
        pre = pre_next
        if w + 1 < len(work):
            pre_next = intra(*work[w + 1])
        inter(i, c, pre)
    y = x + _dot(o_scr[...], wo_ref[...])
    y_ref[...] = y.reshape(nb, tq, D_MODEL)


def _hgrn_layer(x, g, w_in, lb_raw, out_norm, w_o, seg, s0, *, nb, tq, layer):
    b, t_len, _ = x.shape
    n_hgrn = lb_raw.shape[0]
    st_blk = pl.BlockSpec((nb, HG_HEADS, HG_EXPAND, HG_EXPAND), lambda bi, ti: (bi, 0, 0, 0))
    return pl.pallas_call(
        functools.partial(_hgrn_kernel, nb=nb, tq=tq, layer=layer),
        grid=(b // nb, t_len // tq),
        in_specs=[
            pl.BlockSpec((nb, tq, D_MODEL), lambda bi, ti: (bi, ti, 0)),
            _resident((1, D_MODEL)),
            _resident((D_MODEL, 2 * HG_F + 2 * D_MODEL)),
            _resident((n_hgrn, HG_F)),
            _resident((1, HG_EXPAND)),
            _resident((D_MODEL, D_MODEL)),
            _resident((N_SEG * CHUNK, 3 * CHUNK)),
            st_blk,
        ],
        out_specs=[
            pl.BlockSpec((nb, tq, D_MODEL), lambda bi, ti: (bi, ti, 0)),
            st_blk,
        ],
        out_shape=[
            jax.ShapeDtypeStruct((b, t_len, D_MODEL), F32),
            jax.ShapeDtypeStruct((b, HG_HEADS, HG_EXPAND, HG_EXPAND), F32),
        ],
        scratch_shapes=[
            pltpu.VMEM((nb * tq, 2 * HG_F + 2 * D_MODEL), F32),
            pltpu.VMEM((nb * tq, D_MODEL), BF16),
        ],
        compiler_params=pltpu.CompilerParams(
            dimension_semantics=("arbitrary", "arbitrary"), vmem_limit_bytes=VMEM_LIMIT_BYTES),
        name="hgrn_layer",
    )(x, g, w_in, lb_raw, out_norm, w_o, seg, s0)


def kernel(x_prompt, x_sample, cache_k, cache_v, state_s, mixer_norm, mlp_norm, attn_w_qkv,
           attn_w_o, attn_sinks, hgrn_w_in, hgrn_lb, hgrn_out_norm, hgrn_w_o, mlp_w_up,
           mlp_w_down, final_norm):
    bp, tp, _ = x_prompt.shape
    bs, ts, _ = x_sample.shape
    assert cache_k.shape[2] == WINDOW and ts % CHUNK == 0 and tp % CHUNK == 0

    w_qkv = attn_w_qkv.astype(BF16)
    w_ao = attn_w_o.astype(BF16)
    w_in = hgrn_w_in.astype(BF16)
    w_ho = hgrn_w_o.astype(BF16)
    w_up = mlp_w_up.astype(BF16)
    w_down = mlp_w_down.astype(BF16)
    lb_raw = hgrn_lb.astype(F32)
    seg = jnp.asarray(_segment_matrix(), BF16)
    final_g = final_norm.reshape(1, D_MODEL)

    rope_p = _rope_tables(jnp.arange(tp, dtype=F32))
    rope_s = _rope_tables(PAST_LEN + jnp.arange(ts, dtype=F32))
    tq_p = _row_block(tp, 256)
    zero_cache = jnp.zeros((bp, WINDOW, KV_DIM), F32)
    zero_state = jnp.zeros((bp, HG_HEADS, HG_EXPAND, HG_EXPAND), F32)

    yp, ys = x_prompt, x_sample
    kp_l, vp_l, sp_l, ks_l, vs_l, ss_l = [], [], [], [], [], []
    for i in range(DEPTH):
        j = i // 2
        g = mixer_norm[i].reshape(1, D_MODEL)
        if i % 2 == 0:
            yp, kp, vp = _attn_layer(yp, g, w_qkv[j], w_ao[j], attn_sinks[j], rope_p,
                                     zero_cache, zero_cache, nb=1, tq=tq_p, pos0=0)
            ys, kn, vn = _attn_layer(ys, g, w_qkv[j], w_ao[j], attn_sinks[j], rope_s,
                                     cache_k[j].reshape(bs, WINDOW, KV_DIM),
                                     cache_v[j].reshape(bs, WINDOW, KV_DIM),
                                     nb=1, tq=ts, pos0=PAST_LEN)
            kp_l.append(kp.reshape(bp, WINDOW, N_KV, HEAD_DIM))
            vp_l.append(vp.reshape(bp, WINDOW, N_KV, HEAD_DIM))
            ks_l.append(kn.reshape(bs, WINDOW, N_KV, HEAD_DIM))
            vs_l.append(vn.reshape(bs, WINDOW, N_KV, HEAD_DIM))
        else:
            onorm = hgrn_out_norm[j].reshape(1, HG_EXPAND)
            yp, sp = _hgrn_layer(yp, g, w_in[j], lb_raw, onorm, w_ho[j], seg, zero_state,
                                 nb=1, tq=tq_p, layer=j)
            ys, sn = _hgrn_layer(ys, g, w_in[j], lb_raw, onorm, w_ho[j], seg, state_s[j],
                                 nb=1, tq=ts, layer=j)
            sp_l.append(sp)
            ss_l.append(sn)
        gm = mlp_norm[i].reshape(1, D_MODEL)
        final = i == DEPTH - 1
        yp = _mlp(yp.reshape(bp * tp, D_MODEL), gm, w_up[i], w_down[i], final_g,
                  final).reshape(bp, tp, D_MODEL)
        ys = _mlp(ys.reshape(bs * ts, D_MODEL), gm, w_up[i], w_down[i], final_g,
                  final).reshape(bs, ts, D_MODEL)
    return (yp, ys, jnp.stack(kp_l), jnp.stack(vp_l), jnp.stack(sp_l),
            jnp.stack(ks_l), jnp.stack(vs_l), jnp.stack(ss_l))
```

```python
import functools
import math

import jax
import jax.numpy as jnp
import numpy as np
from jax import lax
from jax.experimental import pallas as pl
from jax.experimental.pallas import tpu as pltpu

D_MODEL = 1024
DEPTH = 4
PAST_LEN = 2048
CHUNK = 64
N_HEADS = 16
N_KV = 4
HEAD_DIM = 64
GROUP = N_HEADS // N_KV
ROT_DIM = HEAD_DIM // 4
ROPE_THETA = 500000.0
WINDOW = 128
Q_DIM = N_HEADS * HEAD_DIM
KV_DIM = N_KV * HEAD_DIM
HG_EXPAND = 128
HG_HEADS = D_MODEL // HG_EXPAND
HG_F = HG_HEADS * HG_EXPAND
HG_IN = 2 * HG_F + 2 * D_MODEL
D_FF = 4 * D_MODEL
EPS = 1e-5

LANES = 128
VMEM_LIMIT_BYTES = 56 * 1024 * 1024

KEYS = WINDOW + CHUNK
N_LEVELS = 6
assert 1 << N_LEVELS == CHUNK
N_SEG = N_LEVELS + 2
N_SLAB = Q_DIM // LANES
FF_BLOCK = 1024
PROJ_BLOCK = 1024
Q_BLOCK = 512
LOG2_E = math.log2(math.e)

F32 = jnp.float32
BF16 = jnp.bfloat16


def _rms(x, g):
    ms = jnp.mean(x * x, axis=-1, keepdims=True)
    return x * lax.rsqrt(ms + EPS) * g


def _dot(a, b):
    return jnp.dot(a, b, preferred_element_type=F32)


def _dot_nt(a, b):
    return lax.dot_general(a, b, (((1,), (1,)), ((), ())), preferred_element_type=F32)


def _dot_tn(a, b):
    return lax.dot_general(a, b, (((0,), (0,)), ((), ())), preferred_element_type=F32)


def _block_diag(a):
    w = a.shape[1] // 2
    z = jnp.zeros_like(a[:, :w])
    return jnp.concatenate([jnp.concatenate([a[:, :w], z], axis=1),
                            jnp.concatenate([z, a[:, w:]], axis=1)], axis=0)


def _resident(shape):
    nd = len(shape)
    return pl.BlockSpec(shape, lambda *_: (0,) * nd, pipeline_mode=pl.Buffered(1))


def _row_block(total, want):
    blk = min(total, want)
    while total % blk:
        blk -= CHUNK
    return blk


def _run(pieces):
    for piece in pieces:
        piece()


def _interleave(slots, fillers):
    for k, slot in enumerate(slots):
        slot()
        if k < len(fillers):
            _run(fillers[k])


def _mlp_pieces(x_fn, g_ref, wu_ref, wd_ref, fg_ref, out_fn, *, final):
    st = {}

    def up(j):
        if j == 0:
            st["x"] = x_fn()
            st["h"] = _rms(st["x"], g_ref[...]).astype(BF16)
            st["acc"] = st["x"]
        u = _dot(st["h"], wu_ref[:, j * FF_BLOCK:(j + 1) * FF_BLOCK])
        st[j] = jnp.square(jnp.maximum(u, 0.0)).astype(BF16)

    def down(j):
        st["acc"] = st["acc"] + _dot(st.pop(j), wd_ref[j * FF_BLOCK:(j + 1) * FF_BLOCK, :])
        if j == D_FF // FF_BLOCK - 1:
            acc = st["acc"]
            out_fn(_rms(acc, fg_ref[...]) if final else acc)

    n = D_FF // FF_BLOCK
    return ([functools.partial(up, j) for j in range(n)],
            [functools.partial(down, j) for j in range(n)])


def _mlp_kernel(x_ref, g_ref, wu_ref, wd_ref, fg_ref, o_ref, *, final):
    def store(v):
        o_ref[...] = v

    ups, downs = _mlp_pieces(lambda: x_ref[...], g_ref, wu_ref, wd_ref, fg_ref, store, final=final)
    for up, down in zip(ups, downs):
        up()
        down()


def _mlp(x, g, w_up, w_down, final_g, final):
    m = x.shape[0]
    tm = _row_block(m, 512)
    return pl.pallas_call(
        functools.partial(_mlp_kernel, final=final),
        grid=(m // tm,),
        in_specs=[
            pl.BlockSpec((tm, D_MODEL), lambda i: (i, 0)),
            _resident((1, D_MODEL)),
            _resident((D_MODEL, D_FF)),
            _resident((D_FF, D_MODEL)),
            _resident((1, D_MODEL)),
        ],
        out_specs=pl.BlockSpec((tm, D_MODEL), lambda i: (i, 0)),
        out_shape=jax.ShapeDtypeStruct((m, D_MODEL), F32),
        compiler_params=pltpu.CompilerParams(
            dimension_semantics=("arbitrary",), vmem_limit_bytes=VMEM_LIMIT_BYTES),
        name="mlp",
    )(x, g, w_up, w_down, final_g)


def _swa_project_pieces(x_fn, g_ref, wqkv_ref, tabs, ck_ref, cv_ref, bufs, *, nb, tq):
    q_scr, ka_scr, kb_scr, va_scr, vb_scr = bufs
    cos_ref, sa_ref, sb_ref = tabs
    st = {}

    def hidden():
        if "h" not in st:
            st["h"] = _rms(x_fn(), g_ref[...]).astype(BF16)
        return st["h"]

    def rope(xs):
        return (xs * cos_ref[...] + pltpu.roll(xs, ROT_DIM // 2, 1) * sa_ref[...]
                + pltpu.roll(xs, LANES - ROT_DIM // 2, 1) * sb_ref[...])

    scale = HEAD_DIM ** -0.5

    def q_piece(j):
        qb = _dot(hidden(), wqkv_ref[:, j * Q_BLOCK:(j + 1) * Q_BLOCK])
        for i in range(nb):
            rs = slice(i * tq, (i + 1) * tq)
            for s in range(Q_BLOCK // LANES):
                qs = rope(qb[rs, s * LANES:(s + 1) * LANES]) * scale
                c0 = j * Q_BLOCK + s * LANES
                q_scr[rs, c0:c0 + LANES] = qs.astype(BF16)

    def kv_piece():
        kv = _dot(hidden(), wqkv_ref[:, Q_DIM:Q_DIM + 2 * KV_DIM])
        low = lax.broadcasted_iota(jnp.int32, (1, LANES), 1) < HEAD_DIM
        for i in range(nb):
            rs = slice(i * tq, (i + 1) * tq)
            for s in range(KV_DIM // LANES):
                ls = slice(s * LANES, (s + 1) * LANES)
                knew = rope(kv[rs, ls])
                vnew = kv[rs, KV_DIM + s * LANES:KV_DIM + (s + 1) * LANES]
                kall = jnp.concatenate([ck_ref[i, :, ls], knew], axis=0)
                vall = jnp.concatenate([cv_ref[i, :, ls], vnew], axis=0)
                ck_ref[i, :, ls] = kall[tq:tq + WINDOW]
                cv_ref[i, :, ls] = vall[tq:tq + WINDOW]
                krot = pltpu.roll(kall, HEAD_DIM, 1)
                vrot = pltpu.roll(vall, HEAD_DIM, 1)
                zero = jnp.zeros_like(kall)
                ka_scr[i, 2 * s] = jnp.where(low, kall, zero).astype(BF16)
                kb_scr[i, 2 * s] = jnp.where(low, zero, krot).astype(BF16)
                ka_scr[i, 2 * s + 1] = jnp.where(low, krot, zero).astype(BF16)
                kb_scr[i, 2 * s + 1] = jnp.where(low, zero, kall).astype(BF16)
                va_scr[i, 2 * s] = jnp.where(low, vall, zero).astype(BF16)
                vb_scr[i, 2 * s] = jnp.where(low, zero, vrot).astype(BF16)
                va_scr[i, 2 * s + 1] = jnp.where(low, vrot, zero).astype(BF16)
                vb_scr[i, 2 * s + 1] = jnp.where(low, zero, vall).astype(BF16)

    return [functools.partial(q_piece, j) for j in range(Q_DIM // Q_BLOCK)] + [kv_piece]


def _swa_core_slots(bufs, o_scr, sink_ref, key_pos0, *, nb, tq):
    q_scr, ka_scr, kb_scr, va_scr, vb_scr = bufs
    nch = tq // CHUNK
    work = [(i, c) for i in range(nb) for c in range(nch)]
    key_iota = lax.broadcasted_iota(jnp.int32, (1, KEYS), 1)
    st = {}

    def softmax_sink(s, valid, sink):
        s = jnp.where(valid, s, -jnp.inf)
        m = jnp.maximum(jnp.max(s, axis=-1, keepdims=True), sink)
        e = jnp.exp(s - m)
        den = jnp.sum(e, axis=-1, keepdims=True) + jnp.exp(sink - m)
        return (e / den).astype(BF16)

    def scores(w):
        i, c = work[w]
        r0, k0 = i * tq + c * CHUNK, c * CHUNK
        out = []
        for slab in range(N_SLAB):
            hh = slab // (GROUP // 2)
            qs = q_scr[r0:r0 + CHUNK, slab * LANES:(slab + 1) * LANES]
            out.append((_dot_nt(qs, ka_scr[i, hh, k0:k0 + KEYS, :]),
                        _dot_nt(qs, kb_scr[i, hh, k0:k0 + KEYS, :])))
        st[w] = out

    def attend(w):
        i, c = work[w]
        r0, k0 = i * tq + c * CHUNK, c * CHUNK
        sc = st.pop(w)
        valid = (key_pos0 + k0 + key_iota) >= 0
        probs = [(softmax_sink(sc[s][0], valid, sink_ref[2 * s]),
                  softmax_sink(sc[s][1], valid, sink_ref[2 * s + 1])) for s in range(N_SLAB)]
        for slab in range(N_SLAB):
            hh = slab // (GROUP // 2)
            o = (_dot(probs[slab][0], va_scr[i, hh, k0:k0 + KEYS, :])
                 + _dot(probs[slab][1], vb_scr[i, hh, k0:k0 + KEYS, :]))
            o_scr[r0:r0 + CHUNK, slab * LANES:(slab + 1) * LANES] = o.astype(BF16)

    def slot(w):
        if w < len(work):
            scores(w)
        if w > 0:
            attend(w - 1)

    return [functools.partial(slot, w) for w in range(len(work) + 1)]


def _attn_kernel(sink_ref, x_ref, g_ref, wqkv_ref, wo_ref, cos_ref, sa_ref, sb_ref,
                 ck0_ref, cv0_ref, y_ref, nk_ref, nv_ref,
                 q_scr, ka_scr, kb_scr, va_scr, vb_scr, o_scr, *, nb, tq, pos0):
    t = pl.program_id(1)

    @pl.when(t == 0)
    def _():
        nk_ref[...] = ck0_ref[...]
        nv_ref[...] = cv0_ref[...]

    bufs = (q_scr, ka_scr, kb_scr, va_scr, vb_scr)
    x = x_ref[...].reshape(nb * tq, D_MODEL)
    _run(_swa_project_pieces(lambda: x, g_ref, wqkv_ref, (cos_ref, sa_ref, sb_ref),
                             nk_ref, nv_ref, bufs, nb=nb, tq=tq))
    _run(_swa_core_slots(bufs, o_scr, sink_ref, pos0 + t * tq - WINDOW, nb=nb, tq=tq))
    y = x + _dot(o_scr[...], wo_ref[...])
    y_ref[...] = y.reshape(nb, tq, D_MODEL)


def _swa_scratch(nb, tq):
    kvlen = WINDOW + tq
    return [pltpu.VMEM((nb * tq, Q_DIM), BF16)] + [
        pltpu.VMEM((nb, N_KV, kvlen, LANES), BF16) for _ in range(4)]


def _attn_layer(x, g, w_qkv, w_o, sinks, rope_tabs, cache_k, cache_v, *, nb, tq, pos0):
    b, t_len, _ = x.shape
    cos, sa, sb = rope_tabs
    cache_blk = pl.BlockSpec((nb, WINDOW, KV_DIM), lambda bi, ti: (bi, 0, 0))
    tab_blk = pl.BlockSpec((tq, LANES), lambda bi, ti: (ti, 0))
    return pl.pallas_call(
        functools.partial(_attn_kernel, nb=nb, tq=tq, pos0=pos0),
        grid=(b // nb, t_len // tq),
        in_specs=[
            pl.BlockSpec(memory_space=pltpu.SMEM),
            pl.BlockSpec((nb, tq, D_MODEL), lambda bi, ti: (bi, ti, 0)),
            _resident((1, D_MODEL)),
            _resident((D_MODEL, Q_DIM + 2 * KV_DIM)),
            _resident((Q_DIM, D_MODEL)),
            tab_blk, tab_blk, tab_blk,
            cache_blk, cache_blk,
        ],
        out_specs=[
            pl.BlockSpec((nb, tq, D_MODEL), lambda bi, ti: (bi, ti, 0)),
            cache_blk, cache_blk,
        ],
        out_shape=[
            jax.ShapeDtypeStruct((b, t_len, D_MODEL), F32),
            jax.ShapeDtypeStruct((b, WINDOW, KV_DIM), F32),
            jax.ShapeDtypeStruct((b, WINDOW, KV_DIM), F32),
        ],
        scratch_shapes=_swa_scratch(nb, tq) + [pltpu.VMEM((nb * tq, Q_DIM), BF16)],
        compiler_params=pltpu.CompilerParams(
            dimension_semantics=("arbitrary", "arbitrary"), vmem_limit_bytes=VMEM_LIMIT_BYTES),
        name="swa_layer",
    )(sinks, x, g, w_qkv, w_o, cos, sa, sb, cache_k, cache_v)


def _rope_tables(pos):
    half = ROT_DIM // 2
    inv_freq = ROPE_THETA ** (-(jnp.arange(half, dtype=F32) * 2.0) / ROT_DIM)
    ang = pos[:, None] * inv_freq[None, :]
    cos, sin = jnp.cos(ang), jnp.sin(ang)
    n = pos.shape[0]
    rest = HEAD_DIM - ROT_DIM
    c64 = jnp.concatenate([cos, cos, jnp.ones((n, rest), F32)], axis=1)
    sa64 = jnp.concatenate([jnp.zeros((n, half), F32), sin, jnp.zeros((n, rest), F32)], axis=1)
    sb64 = jnp.concatenate([-sin, jnp.zeros((n, half + rest), F32)], axis=1)
    rep = LANES // HEAD_DIM
    return tuple(jnp.tile(a, (1, rep)) for a in (c64, sa64, sb64))


def _segment_matrix():
    seg = np.zeros((N_SEG, CHUNK, CHUNK), np.float32)
    for l in range(N_LEVELS):
        n = 1 << l
        for t in range(CHUNK):
            mid = (t // (2 * n)) * 2 * n + n
            if t >= mid:
                seg[l, t, mid:t + 1] = 1.0
            else:
                seg[l, t, t + 1:mid] = 1.0
    for t in range(CHUNK):
        seg[N_LEVELS, t, :t + 1] = 1.0
        seg[N_LEVELS + 1, t, t + 1:] = 1.0
    seg = seg.reshape(N_SEG * CHUNK, CHUNK)
    return np.concatenate([seg, seg, seg], axis=1)


def _hgrn_project_pieces(x_fn, g_ref, win_ref, z_scr):
    st = {}

    def piece(j):
        if "h" not in st:
            st["h"] = _rms(x_fn(), g_ref[...]).astype(BF16)
        cs_ = slice(j * PROJ_BLOCK, (j + 1) * PROJ_BLOCK)
        z_scr[:, cs_] = _dot(st["h"], win_ref[:, cs_])

    return [functools.partial(piece, j) for j in range(HG_IN // PROJ_BLOCK)]


def _hgrn_core_slots(z_scr, st_ref, o_scr, lbraw_ref, onorm_ref, seg_ref, *, nb, tq, layer):
    nch = tq // CHUNK
    work = [(i, c) for i in range(nb) for c in range(nch)]
    st = {}

    def consts():
        if "lb" in st:
            return
        lbraw = lbraw_ref[...]
        e = jnp.exp(lbraw - jnp.max(lbraw, axis=0, keepdims=True))
        sm = e / jnp.sum(e, axis=0, keepdims=True)
        cs0 = sm[0:1]
        cs = cs0
        for r in range(1, layer + 1):
            cs = cs + sm[r:r + 1]
        st["lb"] = cs - cs0
        st["one_m_lb"] = 1.0 - st["lb"]
        ti = lax.broadcasted_iota(jnp.int32, (CHUNK, 2 * CHUNK), 0)
        si = lax.broadcasted_iota(jnp.int32, (CHUNK, 2 * CHUNK), 1) & (CHUNK - 1)
        masks = [ti == si]
        for l in range(N_LEVELS):
            masks.append(((ti >> (l + 1)) == (si >> (l + 1)))
                         & (((ti >> l) & 1) == 1) & (((si >> l) & 1) == 0))
        st["masks"] = masks

    def intra(w):
        consts()
        i, c = work[w]
        lb, one_m_lb, masks = st["lb"], st["one_m_lb"], st["masks"]
        r0 = i * tq + c * CHUNK
        zq = z_scr[r0:r0 + CHUNK, 0:HG_F]
        zf = z_scr[r0:r0 + CHUNK, HG_F:2 * HG_F]
        zi = z_scr[r0:r0 + CHUNK, 2 * HG_F:2 * HG_F + D_MODEL]
        zg = z_scr[r0:r0 + CHUNK, 2 * HG_F + D_MODEL:]
        q = zq * (1.0 / (1.0 + jnp.exp(-zq)))
        et = jnp.exp(-jnp.abs(zf))
        rt = 1.0 / (1.0 + et)
        nonneg = zf >= 0.0
        sig_pos = jnp.where(nonneg, rt, et * rt)
        sig_neg = jnp.where(nonneg, et * rt, rt)
        logf = jnp.log(lb + one_m_lb * sig_pos)
        kk = one_m_lb * sig_neg
        gate = zg * (1.0 / (1.0 + jnp.exp(-zg)))
        l2f = logf * LOG2_E
        hi = l2f.astype(BF16)
        r1 = l2f - hi.astype(F32)
        mid = r1.astype(BF16)
        lo = (r1 - mid.astype(F32)).astype(BF16)
        dall = _dot(seg_ref[...], jnp.concatenate([hi, mid, lo], axis=0))
        eall = jnp.exp2(dall)
        pairs = []
        for p in range(HG_HEADS // 2):
            cs_ = slice(2 * p * HG_EXPAND, (2 * p + 2) * HG_EXPAND)
            qp, kp = q[:, cs_], kk[:, cs_]
            sc = jnp.where(masks[0], _dot_nt(qp.astype(BF16), _block_diag(kp.astype(BF16))), 0.0)
            for l in range(N_LEVELS):
                el = eall[l * CHUNK:(l + 1) * CHUNK, cs_]
                pr = _dot_nt((qp * el).astype(BF16), _block_diag((kp * el).astype(BF16)))
                sc = jnp.where(masks[l + 1], pr, sc)
            eb = eall[N_LEVELS * CHUNK:(N_LEVELS + 1) * CHUNK, cs_]
            er = eall[(N_LEVELS + 1) * CHUNK:(N_LEVELS + 2) * CHUNK, cs_]
            eb_last = jnp.broadcast_to(eb[CHUNK - 1:CHUNK, :], (8, 2 * HG_EXPAND))
            pairs.append(dict(
                sc=sc.astype(BF16), qe=(qp * eb).astype(BF16), ke=(kp * er).astype(BF16),
                v=zi[:, cs_].astype(BF16), gate=gate[:, cs_],
                eb_col=jnp.transpose(eb_last)[:, 0:1]))
        st[w] = pairs

    def inter(w):
        i, c = work[w]
        pairs = st.pop(w)
        onorm = onorm_ref[...]
        r0 = i * tq + c * CHUNK
        states = [st_ref[i, hh] for hh in range(HG_HEADS)]
        o_list = []
        for p, pd in enumerate(pairs):
            sbd = _block_diag(jnp.concatenate(
                [states[2 * p].astype(BF16), states[2 * p + 1].astype(BF16)], axis=1))
            o_list.append(_dot(pd["sc"], _block_diag(pd["v"])) + _dot(pd["qe"], sbd))
        new_states = []
        for hh in range(HG_HEADS):
            pd, cs_ = pairs[hh // 2], slice((hh % 2) * HG_EXPAND, (hh % 2 + 1) * HG_EXPAND)
            new_states.append(pd["eb_col"][cs_] * states[hh]
                              + _dot_tn(pd["ke"][:, cs_], pd["v"][:, cs_]))
        for hh in range(HG_HEADS):
            cs_ = slice((hh % 2) * HG_EXPAND, (hh % 2 + 1) * HG_EXPAND)
            o = o_list[hh // 2][:, cs_]
            on = o * lax.rsqrt(jnp.mean(o * o, axis=-1, keepdims=True) + EPS) * onorm
            st_ref[i, hh] = new_states[hh]
            o_scr[r0:r0 + CHUNK, hh * HG_EXPAND:(hh + 1) * HG_EXPAND] = (
                on * pairs[hh // 2]["gate"][:, cs_]).astype(BF16)

    def slot(w):
        if w < len(work):
            intra(w)
        if w > 0:
            inter(w - 1)

    return [functools.partial(slot, w) for w in range(len(work) + 1)]


def _hgrn_kernel(x_ref, g_ref, win_ref, lbraw_ref, onorm_ref, wo_ref, seg_ref, s0_ref,
                 y_ref, st_ref, z_scr, o_scr, *, nb, tq, layer):
    t = pl.program_id(1)

    @pl.when(t == 0)
    def _():
        st_ref[...] = s0_ref[...]

    x = x_ref[...].reshape(nb * tq, D_MODEL)
    _run(_hgrn_project_pieces(lambda: x, g_ref, win_ref, z_scr))
    _run(_hgrn_core_slots(z_scr, st_ref, o_scr, lbraw_ref, onorm_ref, seg_ref,
                          nb=nb, tq=tq, layer=layer))
    y = x + _dot(o_scr[...], wo_ref[...])
    y_ref[...] = y.reshape(nb, tq, D_MODEL)


def _hgrn_layer(x, g, w_in, lb_raw, out_norm, w_o, seg, s0, *, nb, tq, layer):
    b, t_len, _ = x.shape
    n_hgrn = lb_raw.shape[0]
    st_blk = pl.BlockSpec((nb, HG_HEADS, HG_EXPAND, HG_EXPAND), lambda bi, ti: (bi, 0, 0, 0))
    return pl.pallas_call(
        functools.partial(_hgrn_kernel, nb=nb, tq=tq, layer=layer),
        grid=(b // nb, t_len // tq),
        in_specs=[
            pl.BlockSpec((nb, tq, D_MODEL), lambda bi, ti: (bi, ti, 0)),
            _resident((1, D_MODEL)),
            _resident((D_MODEL, HG_IN)),
            _resident((n_hgrn, HG_F)),
            _resident((1, HG_EXPAND)),
            _resident((D_MODEL, D_MODEL)),
            _resident((N_SEG * CHUNK, 3 * CHUNK)),
            st_blk,
        ],
        out_specs=[
            pl.BlockSpec((nb, tq, D_MODEL), lambda bi, ti: (bi, ti, 0)),
            st_blk,
        ],
        out_shape=[
            jax.ShapeDtypeStruct((b, t_len, D_MODEL), F32),
            jax.ShapeDtypeStruct((b, HG_HEADS, HG_EXPAND, HG_EXPAND), F32),
        ],
        scratch_shapes=[
            pltpu.VMEM((nb * tq, HG_IN), F32),
            pltpu.VMEM((nb * tq, D_MODEL), BF16),
        ],
        compiler_params=pltpu.CompilerParams(
            dimension_semantics=("arbitrary", "arbitrary"), vmem_limit_bytes=VMEM_LIMIT_BYTES),
        name="hgrn_layer",
    )(x, g, w_in, lb_raw, out_norm, w_o, seg, s0)


def _pipeline_fillers(a_pieces, ups, downs, tail, n_slots):
    n_ff = len(ups)
    fillers = [[] for _ in range(n_slots)]
    gaps = n_slots - 1
    for j, piece in enumerate(a_pieces):
        fillers[min(j, gaps - 1)].append(piece)
    for j in range(n_ff):
        fillers[min(j, gaps - 1)].append(ups[j])
        fillers[min(j + 1, gaps)].append(downs[j])
    fillers[gaps].extend(tail)
    return fillers


def _fused_swa_kernel(sink_ref, xa_ref, xb_ref, g_ref, wqkv_ref, wo_ref, cos_ref, sa_ref, sb_ref,
                      gm_ref, wu_ref, wd_ref, fg_ref, out_ref, nk_ref, nv_ref,
                      *scr, tq, nt, nblk, final):
    set0, set1 = scr[0:5], scr[5:10]
    y1_0, y1_1, o_scr, ck_scr, cv_scr = scr[10:15]
    s = pl.program_id(0)
    t_a = jnp.minimum(s, nblk - 1) % nt
    t_b = jnp.clip(s - 1, 0, nblk - 1) % nt

    @pl.when(s == 0)
    def _():
        for ref in set1 + (y1_0,):
            ref[...] = jnp.zeros_like(ref)

    @pl.when(t_a == 0)
    def _():
        ck_scr[...] = jnp.zeros_like(ck_scr)
        cv_scr[...] = jnp.zeros_like(cv_scr)

    def step(wset, rset, y1w, y1r):
        a_pieces = _swa_project_pieces(lambda: xa_ref[...], g_ref, wqkv_ref,
                                       (cos_ref, sa_ref, sb_ref), ck_scr, cv_scr, wset,
                                       nb=1, tq=tq)
        slots = _swa_core_slots(rset, o_scr, sink_ref, t_b * tq - WINDOW, nb=1, tq=tq)

        def store(v):
            out_ref[...] = v

        ups, downs = _mlp_pieces(lambda: y1r[...], gm_ref, wu_ref, wd_ref, fg_ref, store,
                                 final=final)

        def out_proj():
            y1w[...] = xb_ref[...] + _dot(o_scr[...], wo_ref[...])

        _interleave(slots, _pipeline_fillers(a_pieces, ups, downs, [out_proj], len(slots)))

    @pl.when(s % 2 == 0)
    def _():
        step(set0, set1, y1_1, y1_0)

    @pl.when(s % 2 == 1)
    def _():
        step(set1, set0, y1_0, y1_1)

    @pl.when(t_a == nt - 1)
    def _():
        nk_ref[...] = ck_scr[...]
        nv_ref[...] = cv_scr[...]


def _fused_hgrn_kernel(xa_ref, xb_ref, g_ref, win_ref, lbraw_ref, onorm_ref, wo_ref, seg_ref,
                       gm_ref, wu_ref, wd_ref, fg_ref, out_ref, st_ref,
                       z0, z1, y1_0, y1_1, o_scr, st_scr, *, tq, nt, nblk, layer, final):
    s = pl.program_id(0)
    t_b = jnp.clip(s - 1, 0, nblk - 1) % nt

    @pl.when(s == 0)
    def _():
        z1[...] = jnp.zeros_like(z1)
        y1_0[...] = jnp.zeros_like(y1_0)

    @pl.when(t_b == 0)
    def _():
        st_scr[...] = jnp.zeros_like(st_scr)

    def step(zw, zr, y1w, y1r):
        a_pieces = _hgrn_project_pieces(lambda: xa_ref[...], g_ref, win_ref, zw)
        slots = _hgrn_core_slots(zr, st_scr, o_scr, lbraw_ref, onorm_ref, seg_ref,
                                 nb=1, tq=tq, layer=layer)

        def store(v):
            out_ref[...] = v

        ups, downs = _mlp_pieces(lambda: y1r[...], gm_ref, wu_ref, wd_ref, fg_ref, store,
                                 final=final)

        def out_proj():
            y1w[...] = xb_ref[...] + _dot(o_scr[...], wo_ref[...])

        _interleave(slots, _pipeline_fillers(a_pieces, ups, downs, [out_proj], len(slots)))

    @pl.when(s % 2 == 0)
    def _():
        step(z0, z1, y1_1, y1_0)

    @pl.when(s % 2 == 1)
    def _():
        step(z1, z0, y1_0, y1_1)

    @pl.when((t_b == nt - 1) & (s >= 1) & (s <= nblk))
    def _():
        st_ref[...] = st_scr[...]


def _fused_layer(kind, x, tq, mixer_args, mlp_args, *, layer, final):
    b, t_len, _ = x.shape
    nt = t_len // tq
    nblk = b * nt
    x2 = x.reshape(b * t_len, D_MODEL)

    def blk_a(s):
        return jnp.minimum(s, nblk - 1)

    def blk_b(s):
        return jnp.clip(s - 1, 0, nblk - 1)

    def blk_c(s):
        return jnp.clip(s - 2, 0, nblk - 1)

    xa_spec = pl.BlockSpec((tq, D_MODEL), lambda s: (blk_a(s), 0))
    xb_spec = pl.BlockSpec((tq, D_MODEL), lambda s: (blk_b(s), 0))
    out_spec = pl.BlockSpec((tq, D_MODEL), lambda s: (blk_c(s), 0))
    mlp_specs = [_resident((1, D_MODEL)), _resident((D_MODEL, D_FF)),
                 _resident((D_FF, D_MODEL)), _resident((1, D_MODEL))]
    y1_scr = [pltpu.VMEM((tq, D_MODEL), F32), pltpu.VMEM((tq, D_MODEL), F32)]
    o_scr = pltpu.VMEM((tq, D_MODEL), BF16)
    params = pltpu.CompilerParams(dimension_semantics=("arbitrary",),
                                  vmem_limit_bytes=VMEM_LIMIT_BYTES)
    y_shape = jax.ShapeDtypeStruct((b * t_len, D_MODEL), F32)

    if kind == "swa":
        g, w_qkv, w_o, sinks, (cos, sa, sb) = mixer_args
        tab = pl.BlockSpec((tq, LANES), lambda s: (blk_a(s) % nt, 0))
        cache_blk = pl.BlockSpec((1, WINDOW, KV_DIM), lambda s: (blk_a(s) // nt, 0, 0))
        cache_shape = jax.ShapeDtypeStruct((b, WINDOW, KV_DIM), F32)
        y, nk, nv = pl.pallas_call(
            functools.partial(_fused_swa_kernel, tq=tq, nt=nt, nblk=nblk, final=final),
            grid=(nblk + 2,),
            in_specs=[pl.BlockSpec(memory_space=pltpu.SMEM), xa_spec, xb_spec,
                      _resident((1, D_MODEL)), _resident((D_MODEL, Q_DIM + 2 * KV_DIM)),
                      _resident((Q_DIM, D_MODEL)), tab, tab, tab] + mlp_specs,
            out_specs=[out_spec, cache_blk, cache_blk],
            out_shape=[y_shape, cache_shape, cache_shape],
            scratch_shapes=(_swa_scratch(1, tq) + _swa_scratch(1, tq) + y1_scr + [o_scr]
                            + [pltpu.VMEM((1, WINDOW, KV_DIM), F32) for _ in range(2)]),
            compiler_params=params,
            name="swa_mlp_layer",
        )(sinks, x2, x2, g, w_qkv, w_o, cos, sa, sb, *mlp_args)
        return y.reshape(b, t_len, D_MODEL), (nk, nv)

    g, w_in, lb_raw, onorm, w_o, seg = mixer_args
    st_blk = pl.BlockSpec((1, HG_HEADS, HG_EXPAND, HG_EXPAND),
                          lambda s: (blk_b(s) // nt, 0, 0, 0))
    y, st = pl.pallas_call(
        functools.partial(_fused_hgrn_kernel, tq=tq, nt=nt, nblk=nblk, layer=layer, final=final),
        grid=(nblk + 2,),
        in_specs=[xa_spec, xb_spec, _resident((1, D_MODEL)), _resident((D_MODEL, HG_IN)),
                  _resident((lb_raw.shape[0], HG_F)), _resident((1, HG_EXPAND)),
                  _resident((D_MODEL, D_MODEL)), _resident((N_SEG * CHUNK, 3 * CHUNK))] + mlp_specs,
        out_specs=[out_spec, st_blk],
        out_shape=[y_shape, jax.ShapeDtypeStruct((b, HG_HEADS, HG_EXPAND, HG_EXPAND), F32)],
        scratch_shapes=([pltpu.VMEM((tq, HG_IN), F32), pltpu.VMEM((tq, HG_IN), F32)] + y1_scr
                        + [o_scr, pltpu.VMEM((1, HG_HEADS, HG_EXPAND, HG_EXPAND), F32)]),
        compiler_params=params,
        name="hgrn_mlp_layer",
    )(x2, x2, g, w_in, lb_raw, onorm, w_o, seg, *mlp_args)
    return y.reshape(b, t_len, D_MODEL), (st,)


def kernel(x_prompt, x_sample, cache_k, cache_v, state_s, mixer_norm, mlp_norm, attn_w_qkv,
           attn_w_o, attn_sinks, hgrn_w_in, hgrn_lb, hgrn_out_norm, hgrn_w_o, mlp_w_up,
           mlp_w_down, final_norm):
    bp, tp, _ = x_prompt.shape
    bs, ts, _ = x_sample.shape
    assert cache_k.shape[2] == WINDOW and ts % CHUNK == 0 and tp % CHUNK == 0

    w_qkv = attn_w_qkv.astype(BF16)
    w_ao = attn_w_o.astype(BF16)
    w_in = hgrn_w_in.astype(BF16)
    w_ho = hgrn_w_o.astype(BF16)
    w_up = mlp_w_up.astype(BF16)
    w_down = mlp_w_down.astype(BF16)
    lb_raw = hgrn_lb.astype(F32)
    seg = jnp.asarray(_segment_matrix(), BF16)
    final_g = final_norm.reshape(1, D_MODEL)

    rope_p = _rope_tables(jnp.arange(tp, dtype=F32))
    rope_s = _rope_tables(PAST_LEN + jnp.arange(ts, dtype=F32))
    tq_p = _row_block(tp, 256)
    assert tq_p >= WINDOW

    yp, ys = x_prompt, x_sample
    kp_l, vp_l, sp_l, ks_l, vs_l, ss_l = [], [], [], [], [], []
    for i in range(DEPTH):
        j = i // 2
        g = mixer_norm[i].reshape(1, D_MODEL)
        gm = mlp_norm[i].reshape(1, D_MODEL)
        final = i == DEPTH - 1
        mlp_args = (gm, w_up[i], w_down[i], final_g)
        if i % 2 == 0:
            yp, (kp, vp) = _fused_layer("swa", yp, tq_p,
                                        (g, w_qkv[j], w_ao[j], attn_sinks[j], rope_p),
                                        mlp_args, layer=j, final=final)
            ys, kn, vn = _attn_layer(ys, g, w_qkv[j], w_ao[j], attn_sinks[j], rope_s,
                                     cache_k[j].reshape(bs, WINDOW, KV_DIM),
                                     cache_v[j].reshape(bs, WINDOW, KV_DIM),
                                     nb=1, tq=ts, pos0=PAST_LEN)
            kp_l.append(kp.reshape(bp, WINDOW, N_KV, HEAD_DIM))
            vp_l.append(vp.reshape(bp, WINDOW, N_KV, HEAD_DIM))
            ks_l.append(kn.reshape(bs, WINDOW, N_KV, HEAD_DIM))
            vs_l.append(vn.reshape(bs, WINDOW, N_KV, HEAD_DIM))
        else:
            onorm = hgrn_out_norm[j].reshape(1, HG_EXPAND)
            yp, (sp,) = _fused_layer("hgrn", yp, tq_p,
                                     (g, w_in[j], lb_raw, onorm, w_ho[j], seg),
                                     mlp_args, layer=j, final=final)
            ys, sn = _hgrn_layer(ys, g, w_in[j], lb_raw, onorm, w_ho[j], seg, state_s[j],
                                 nb=1, tq=ts, layer=j)
            sp_l.append(sp)
            ss_l.append(sn)
        ys = _mlp(ys.reshape(bs * ts, D_MODEL), gm, w_up[i], w_down[i], final_g,
                  final).reshape(bs, ts, D_MODEL)
    return (yp, ys, jnp.stack(kp_l), jnp.stack(vp_l), jnp.stack(sp_l),
            jnp.stack(ks_l), jnp.stack(vs_l), jnp.stack(ss_l))
```

```python
import functools
import math

import jax
import jax.numpy as jnp
import numpy as np
from jax import lax
from jax.experimental import pallas as pl
from jax.experimental.pallas import tpu as pltpu

D_MODEL = 1024
DEPTH = 4
PAST_LEN = 2048
CHUNK = 64
N_HEADS = 16
N_KV = 4
HEAD_DIM = 64
GROUP = N_HEADS // N_KV
ROT_DIM = HEAD_DIM // 4
ROPE_THETA = 500000.0
WINDOW = 128
Q_DIM = N_HEADS * HEAD_DIM
KV_DIM = N_KV * HEAD_DIM
HG_EXPAND = 128
HG_HEADS = D_MODEL // HG_EXPAND
HG_F = HG_HEADS * HG_EXPAND
HG_IN = 2 * HG_F + 2 * D_MODEL
D_FF = 4 * D_MODEL
EPS = 1e-5

LANES = 128
VMEM_LIMIT_BYTES = 56 * 1024 * 1024

KEYS = WINDOW + CHUNK
N_LEVELS = 6
assert 1 << N_LEVELS == CHUNK
N_SLAB = Q_DIM // LANES
FF_BLOCK = 1024
PROJ_BLOCK = 1024
Q_BLOCK = 512
LOG2_E = math.log2(math.e)

F32 = jnp.float32
BF16 = jnp.bfloat16


def _rms(x, g):
    ms = jnp.mean(x * x, axis=-1, keepdims=True)
    return x * lax.rsqrt(ms + EPS) * g


def _dot(a, b):
    return jnp.dot(a, b, preferred_element_type=F32)


def _dot_nt(a, b):
    return lax.dot_general(a, b, (((1,), (1,)), ((), ())), preferred_element_type=F32)


def _dot_tn(a, b):
    return lax.dot_general(a, b, (((0,), (0,)), ((), ())), preferred_element_type=F32)


def _block_diag(a):
    w = a.shape[1] // 2
    z = jnp.zeros_like(a[:, :w])
    return jnp.concatenate([jnp.concatenate([a[:, :w], z], axis=1),
                            jnp.concatenate([z, a[:, w:]], axis=1)], axis=0)


def _resident(shape):
    nd = len(shape)
    return pl.BlockSpec(shape, lambda *_: (0,) * nd, pipeline_mode=pl.Buffered(1))


def _row_block(total, want):
    blk = min(total, want)
    while total % blk:
        blk -= CHUNK
    return blk


def _run(pieces):
    for piece in pieces:
        piece()


def _interleave(slots, fillers):
    for k, slot in enumerate(slots):
        slot()
        if k < len(fillers):
            _run(fillers[k])


def _mlp_pieces(x_fn, g_ref, wu_ref, wd_ref, fg_ref, out_fn, *, final):
    st = {}

    def up(j):
        if j == 0:
            st["x"] = x_fn()
            st["h"] = _rms(st["x"], g_ref[...]).astype(BF16)
            st["acc"] = st["x"]
        u = _dot(st["h"], wu_ref[:, j * FF_BLOCK:(j + 1) * FF_BLOCK])
        st[j] = jnp.square(jnp.maximum(u, 0.0)).astype(BF16)

    def down(j):
        st["acc"] = st["acc"] + _dot(st.pop(j), wd_ref[j * FF_BLOCK:(j + 1) * FF_BLOCK, :])
        if j == D_FF // FF_BLOCK - 1:
            acc = st["acc"]
            out_fn(_rms(acc, fg_ref[...]) if final else acc)

    n = D_FF // FF_BLOCK
    return ([functools.partial(up, j) for j in range(n)],
            [functools.partial(down, j) for j in range(n)])


def _mlp_kernel(x_ref, g_ref, wu_ref, wd_ref, fg_ref, o_ref, *, final):
    def store(v):
        o_ref[...] = v

    ups, downs = _mlp_pieces(lambda: x_ref[...], g_ref, wu_ref, wd_ref, fg_ref, store, final=final)
    for up, down in zip(ups, downs):
        up()
        down()


def _mlp(x, g, w_up, w_down, final_g, final):
    m = x.shape[0]
    tm = _row_block(m, 512)
    return pl.pallas_call(
        functools.partial(_mlp_kernel, final=final),
        grid=(m // tm,),
        in_specs=[
            pl.BlockSpec((tm, D_MODEL), lambda i: (i, 0)),
            _resident((1, D_MODEL)),
            _resident((D_MODEL, D_FF)),
            _resident((D_FF, D_MODEL)),
            _resident((1, D_MODEL)),
        ],
        out_specs=pl.BlockSpec((tm, D_MODEL), lambda i: (i, 0)),
        out_shape=jax.ShapeDtypeStruct((m, D_MODEL), F32),
        compiler_params=pltpu.CompilerParams(
            dimension_semantics=("arbitrary",), vmem_limit_bytes=VMEM_LIMIT_BYTES),
        name="mlp",
    )(x, g, w_up, w_down, final_g)


def _swa_project_pieces(x_fn, g_ref, wqkv_ref, tabs, ck_ref, cv_ref, bufs, *, nb, tq):
    q_scr, ka_scr, kb_scr, va_scr, vb_scr = bufs
    cos_ref, sa_ref, sb_ref = tabs
    st = {}

    def hidden():
        if "h" not in st:
            st["h"] = _rms(x_fn(), g_ref[...]).astype(BF16)
        return st["h"]

    def rope(xs):
        return (xs * cos_ref[...] + pltpu.roll(xs, ROT_DIM // 2, 1) * sa_ref[...]
                + pltpu.roll(xs, LANES - ROT_DIM // 2, 1) * sb_ref[...])

    scale = HEAD_DIM ** -0.5

    def q_piece(j):
        qb = _dot(hidden(), wqkv_ref[:, j * Q_BLOCK:(j + 1) * Q_BLOCK])
        for i in range(nb):
            rs = slice(i * tq, (i + 1) * tq)
            for s in range(Q_BLOCK // LANES):
                qs = rope(qb[rs, s * LANES:(s + 1) * LANES]) * scale
                c0 = j * Q_BLOCK + s * LANES
                q_scr[rs, c0:c0 + LANES] = qs.astype(BF16)

    def kv_piece():
        kv = _dot(hidden(), wqkv_ref[:, Q_DIM:Q_DIM + 2 * KV_DIM])
        low = lax.broadcasted_iota(jnp.int32, (1, LANES), 1) < HEAD_DIM
        for i in range(nb):
            rs = slice(i * tq, (i + 1) * tq)
            for s in range(KV_DIM // LANES):
                ls = slice(s * LANES, (s + 1) * LANES)
                knew = rope(kv[rs, ls])
                vnew = kv[rs, KV_DIM + s * LANES:KV_DIM + (s + 1) * LANES]
                kall = jnp.concatenate([ck_ref[i, :, ls], knew], axis=0)
                vall = jnp.concatenate([cv_ref[i, :, ls], vnew], axis=0)
                ck_ref[i, :, ls] = kall[tq:tq + WINDOW]
                cv_ref[i, :, ls] = vall[tq:tq + WINDOW]
                krot = pltpu.roll(kall, HEAD_DIM, 1)
                vrot = pltpu.roll(vall, HEAD_DIM, 1)
                zero = jnp.zeros_like(kall)
                ka_scr[i, 2 * s] = jnp.where(low, kall, zero).astype(BF16)
                kb_scr[i, 2 * s] = jnp.where(low, zero, krot).astype(BF16)
                ka_scr[i, 2 * s + 1] = jnp.where(low, krot, zero).astype(BF16)
                kb_scr[i, 2 * s + 1] = jnp.where(low, zero, kall).astype(BF16)
                va_scr[i, 2 * s] = jnp.where(low, vall, zero).astype(BF16)
                vb_scr[i, 2 * s] = jnp.where(low, zero, vrot).astype(BF16)
                va_scr[i, 2 * s + 1] = jnp.where(low, vrot, zero).astype(BF16)
                vb_scr[i, 2 * s + 1] = jnp.where(low, zero, vall).astype(BF16)

    return [functools.partial(q_piece, j) for j in range(Q_DIM // Q_BLOCK)] + [kv_piece]


def _swa_core_slots(bufs, o_scr, sink_ref, key_pos0, *, nb, tq):
    q_scr, ka_scr, kb_scr, va_scr, vb_scr = bufs
    nch = tq // CHUNK
    work = [(i, c) for i in range(nb) for c in range(nch)]
    key_iota = lax.broadcasted_iota(jnp.int32, (1, KEYS), 1)
    st = {}

    def softmax_sink(s, valid, sink):
        s = jnp.where(valid, s, -jnp.inf)
        m = jnp.maximum(jnp.max(s, axis=-1, keepdims=True), sink)
        e = jnp.exp(s - m)
        den = jnp.sum(e, axis=-1, keepdims=True) + jnp.exp(sink - m)
        return (e / den).astype(BF16)

    spk = GROUP // 2

    def scores(w):
        i, c = work[w]
        r0, k0 = i * tq + c * CHUNK, c * CHUNK
        out = []
        for hh in range(N_KV):
            qs = jnp.concatenate(
                [q_scr[r0:r0 + CHUNK, s * LANES:(s + 1) * LANES]
                 for s in range(hh * spk, (hh + 1) * spk)], axis=0)
            out.append((_dot_nt(qs, ka_scr[i, hh, k0:k0 + KEYS, :]),
                        _dot_nt(qs, kb_scr[i, hh, k0:k0 + KEYS, :])))
        st[w] = out

    def attend(w):
        i, c = work[w]
        r0, k0 = i * tq + c * CHUNK, c * CHUNK
        sc = st.pop(w)
        valid = (key_pos0 + k0 + key_iota) >= 0
        for hh in range(N_KV):
            pa, pb = [], []
            for j in range(spk):
                slab = hh * spk + j
                rs = slice(j * CHUNK, (j + 1) * CHUNK)
                pa.append(softmax_sink(sc[hh][0][rs], valid, sink_ref[2 * slab]))
                pb.append(softmax_sink(sc[hh][1][rs], valid, sink_ref[2 * slab + 1]))
            o = (_dot(jnp.concatenate(pa, axis=0), va_scr[i, hh, k0:k0 + KEYS, :])
                 + _dot(jnp.concatenate(pb, axis=0), vb_scr[i, hh, k0:k0 + KEYS, :]))
            for j in range(spk):
                slab = hh * spk + j
                o_scr[r0:r0 + CHUNK, slab * LANES:(slab + 1) * LANES] = (
                    o[j * CHUNK:(j + 1) * CHUNK].astype(BF16))

    def slot(w):
        if w < len(work):
            scores(w)
        if w > 0:
            attend(w - 1)

    return [functools.partial(slot, w) for w in range(len(work) + 1)]


def _attn_kernel(sink_ref, x_ref, g_ref, wqkv_ref, wo_ref, cos_ref, sa_ref, sb_ref,
                 ck0_ref, cv0_ref, y_ref, nk_ref, nv_ref,
                 q_scr, ka_scr, kb_scr, va_scr, vb_scr, o_scr, *, nb, tq, pos0):
    t = pl.program_id(1)

    @pl.when(t == 0)
    def _():
        nk_ref[...] = ck0_ref[...]
        nv_ref[...] = cv0_ref[...]

    bufs = (q_scr, ka_scr, kb_scr, va_scr, vb_scr)
    x = x_ref[...].reshape(nb * tq, D_MODEL)
    _run(_swa_project_pieces(lambda: x, g_ref, wqkv_ref, (cos_ref, sa_ref, sb_ref),
                             nk_ref, nv_ref, bufs, nb=nb, tq=tq))
    _run(_swa_core_slots(bufs, o_scr, sink_ref, pos0 + t * tq - WINDOW, nb=nb, tq=tq))
    y = x + _dot(o_scr[...], wo_ref[...])
    y_ref[...] = y.reshape(nb, tq, D_MODEL)


def _swa_scratch(nb, tq):
    kvlen = WINDOW + tq
    return [pltpu.VMEM((nb * tq, Q_DIM), BF16)] + [
        pltpu.VMEM((nb, N_KV, kvlen, LANES), BF16) for _ in range(4)]


def _attn_layer(x, g, w_qkv, w_o, sinks, rope_tabs, cache_k, cache_v, *, nb, tq, pos0):
    b, t_len, _ = x.shape
    cos, sa, sb = rope_tabs
    cache_blk = pl.BlockSpec((nb, WINDOW, KV_DIM), lambda bi, ti: (bi, 0, 0))
    tab_blk = pl.BlockSpec((tq, LANES), lambda bi, ti: (ti, 0))
    return pl.pallas_call(
        functools.partial(_attn_kernel, nb=nb, tq=tq, pos0=pos0),
        grid=(b // nb, t_len // tq),
        in_specs=[
            pl.BlockSpec(memory_space=pltpu.SMEM),
            pl.BlockSpec((nb, tq, D_MODEL), lambda bi, ti: (bi, ti, 0)),
            _resident((1, D_MODEL)),
            _resident((D_MODEL, Q_DIM + 2 * KV_DIM)),
            _resident((Q_DIM, D_MODEL)),
            tab_blk, tab_blk, tab_blk,
            cache_blk, cache_blk,
        ],
        out_specs=[
            pl.BlockSpec((nb, tq, D_MODEL), lambda bi, ti: (bi, ti, 0)),
            cache_blk, cache_blk,
        ],
        out_shape=[
            jax.ShapeDtypeStruct((b, t_len, D_MODEL), F32),
            jax.ShapeDtypeStruct((b, WINDOW, KV_DIM), F32),
            jax.ShapeDtypeStruct((b, WINDOW, KV_DIM), F32),
        ],
        scratch_shapes=_swa_scratch(nb, tq) + [pltpu.VMEM((nb * tq, Q_DIM), BF16)],
        compiler_params=pltpu.CompilerParams(
            dimension_semantics=("arbitrary", "arbitrary"), vmem_limit_bytes=VMEM_LIMIT_BYTES),
        name="swa_layer",
    )(sinks, x, g, w_qkv, w_o, cos, sa, sb, cache_k, cache_v)


def _rope_tables(pos):
    half = ROT_DIM // 2
    inv_freq = ROPE_THETA ** (-(jnp.arange(half, dtype=F32) * 2.0) / ROT_DIM)
    ang = pos[:, None] * inv_freq[None, :]
    cos, sin = jnp.cos(ang), jnp.sin(ang)
    n = pos.shape[0]
    rest = HEAD_DIM - ROT_DIM
    c64 = jnp.concatenate([cos, cos, jnp.ones((n, rest), F32)], axis=1)
    sa64 = jnp.concatenate([jnp.zeros((n, half), F32), sin, jnp.zeros((n, rest), F32)], axis=1)
    sb64 = jnp.concatenate([-sin, jnp.zeros((n, half + rest), F32)], axis=1)
    rep = LANES // HEAD_DIM
    return tuple(jnp.tile(a, (1, rep)) for a in (c64, sa64, sb64))


def _segment_matrix():
    tril = np.tril(np.ones((CHUNK, CHUNK), np.float32))
    return np.concatenate([tril, tril, tril], axis=1)


def _hgrn_project_pieces(x_fn, g_ref, win_ref, z_scr):
    st = {}

    def piece(j):
        if "h" not in st:
            st["h"] = _rms(x_fn(), g_ref[...]).astype(BF16)
        cs_ = slice(j * PROJ_BLOCK, (j + 1) * PROJ_BLOCK)
        z_scr[:, cs_] = _dot(st["h"], win_ref[:, cs_])

    return [functools.partial(piece, j) for j in range(HG_IN // PROJ_BLOCK)]


def _hgrn_core_slots(z_scr, st_ref, o_scr, lbraw_ref, onorm_ref, seg_ref, *, nb, tq, layer):
    nch = tq // CHUNK
    work = [(i, c) for i in range(nb) for c in range(nch)]
    st = {}

    def consts():
        if "lb" in st:
            return
        lbraw = lbraw_ref[...]
        e = jnp.exp(lbraw - jnp.max(lbraw, axis=0, keepdims=True))
        sm = e / jnp.sum(e, axis=0, keepdims=True)
        cs0 = sm[0:1]
        cs = cs0
        for r in range(1, layer + 1):
            cs = cs + sm[r:r + 1]
        st["lb"] = cs - cs0
        st["one_m_lb"] = 1.0 - st["lb"]
        ti = lax.broadcasted_iota(jnp.int32, (CHUNK, 2 * CHUNK), 0)
        si = lax.broadcasted_iota(jnp.int32, (CHUNK, 2 * CHUNK), 1) & (CHUNK - 1)
        masks = [ti == si]
        for l in range(N_LEVELS):
            masks.append(((ti >> (l + 1)) == (si >> (l + 1)))
                         & (((ti >> l) & 1) == 1) & (((si >> l) & 1) == 0))
        st["masks"] = masks

    def intra(w):
        consts()
        i, c = work[w]
        lb, one_m_lb, masks = st["lb"], st["one_m_lb"], st["masks"]
        r0 = i * tq + c * CHUNK
        zq = z_scr[r0:r0 + CHUNK, 0:HG_F]
        zf = z_scr[r0:r0 + CHUNK, HG_F:2 * HG_F]
        zi = z_scr[r0:r0 + CHUNK, 2 * HG_F:2 * HG_F + D_MODEL]
        zg = z_scr[r0:r0 + CHUNK, 2 * HG_F + D_MODEL:]
        q = zq * (1.0 / (1.0 + jnp.exp(-zq)))
        et = jnp.exp(-jnp.abs(zf))
        rt = 1.0 / (1.0 + et)
        nonneg = zf >= 0.0
        sig_pos = jnp.where(nonneg, rt, et * rt)
        sig_neg = jnp.where(nonneg, et * rt, rt)
        fgate = lb + one_m_lb * sig_pos
        kk = one_m_lb * sig_neg
        gate = zg * (1.0 / (1.0 + jnp.exp(-zg)))
        l2f = jnp.log(fgate) * LOG2_E
        hi = l2f.astype(BF16)
        r1 = l2f - hi.astype(F32)
        mid = r1.astype(BF16)
        lo = (r1 - mid.astype(F32)).astype(BF16)
        b = _dot(seg_ref[...], jnp.concatenate([hi, mid, lo], axis=0))
        row = lax.broadcasted_iota(jnp.int32, (CHUNK, 1), 0)
        elev = [jnp.where((row & 1) == 1, fgate, 1.0)]
        for l in range(1, N_LEVELS):
            n = 1 << l
            if 2 * n >= 8:
                bref = jnp.concatenate(
                    [jnp.broadcast_to(b[blk + n - 1:blk + n, :], (2 * n, HG_F))
                     for blk in range(0, CHUNK, 2 * n)], axis=0)
            else:
                sub = lax.broadcasted_iota(jnp.int32, (8, 1), 0)
                bref = jnp.concatenate(
                    [jnp.where(((sub >> (l + 1)) & 1) == 0,
                               jnp.broadcast_to(b[g + n - 1:g + n, :], (8, HG_F)),
                               jnp.broadcast_to(b[g + 3 * n - 1:g + 3 * n, :], (8, HG_F)))
                     for g in range(0, CHUNK, 8)], axis=0)
            sgn = jnp.where(((row >> l) & 1) == 1, 1.0, -1.0)
            elev.append(jnp.exp2((b - bref) * sgn))
        eb_all = jnp.exp2(b)
        er_all = jnp.exp2(b[CHUNK - 1:CHUNK, :] - b)
        pairs = []
        for p in range(HG_HEADS // 2):
            cs_ = slice(2 * p * HG_EXPAND, (2 * p + 2) * HG_EXPAND)
            qp, kp = q[:, cs_], kk[:, cs_]
            sc = jnp.where(masks[0], _dot_nt(qp.astype(BF16), _block_diag(kp.astype(BF16))), 0.0)
            for l in range(N_LEVELS):
                el = elev[l][:, cs_]
                pr = _dot_nt((qp * el).astype(BF16), _block_diag((kp * el).astype(BF16)))
                sc = jnp.where(masks[l + 1], pr, sc)
            eb, er = eb_all[:, cs_], er_all[:, cs_]
            eb_last = jnp.broadcast_to(eb[CHUNK - 1:CHUNK, :], (8, 2 * HG_EXPAND))
            pairs.append(dict(
                sc=sc.astype(BF16), qe=(qp * eb).astype(BF16), ke=(kp * er).astype(BF16),
                v=zi[:, cs_].astype(BF16), gate=gate[:, cs_],
                eb_col=jnp.transpose(eb_last)[:, 0:1]))
        st[w] = pairs

    def inter(w):
        i, c = work[w]
        pairs = st.pop(w)
        onorm = onorm_ref[...]
        r0 = i * tq + c * CHUNK
        states = [st_ref[i, hh] for hh in range(HG_HEADS)]
        o_list = []
        for p, pd in enumerate(pairs):
            sbd = _block_diag(jnp.concatenate(
                [states[2 * p].astype(BF16), states[2 * p + 1].astype(BF16)], axis=1))
            o_list.append(_dot(pd["sc"], _block_diag(pd["v"])) + _dot(pd["qe"], sbd))
        new_states = []
        for hh in range(HG_HEADS):
            pd, cs_ = pairs[hh // 2], slice((hh % 2) * HG_EXPAND, (hh % 2 + 1) * HG_EXPAND)
            new_states.append(pd["eb_col"][cs_] * states[hh]
                              + _dot_tn(pd["ke"][:, cs_], pd["v"][:, cs_]))
        for hh in range(HG_HEADS):
            cs_ = slice((hh % 2) * HG_EXPAND, (hh % 2 + 1) * HG_EXPAND)
            o = o_list[hh // 2][:, cs_]
            on = o * lax.rsqrt(jnp.mean(o * o, axis=-1, keepdims=True) + EPS) * onorm
            st_ref[i, hh] = new_states[hh]
            o_scr[r0:r0 + CHUNK, hh * HG_EXPAND:(hh + 1) * HG_EXPAND] = (
                on * pairs[hh // 2]["gate"][:, cs_]).astype(BF16)

    def slot(w):
        if w < len(work):
            intra(w)
        if w > 0:
            inter(w - 1)

    return [functools.partial(slot, w) for w in range(len(work) + 1)]


def _hgrn_kernel(x_ref, g_ref, win_ref, lbraw_ref, onorm_ref, wo_ref, seg_ref, s0_ref,
                 y_ref, st_ref, z_scr, o_scr, *, nb, tq, layer):
    t = pl.program_id(1)

    @pl.when(t == 0)
    def _():
        st_ref[...] = s0_ref[...]

    x = x_ref[...].reshape(nb * tq, D_MODEL)
    _run(_hgrn_project_pieces(lambda: x, g_ref, win_ref, z_scr))
    _run(_hgrn_core_slots(z_scr, st_ref, o_scr, lbraw_ref, onorm_ref, seg_ref,
                          nb=nb, tq=tq, layer=layer))
    y = x + _dot(o_scr[...], wo_ref[...])
    y_ref[...] = y.reshape(nb, tq, D_MODEL)


def _hgrn_layer(x, g, w_in, lb_raw, out_norm, w_o, seg, s0, *, nb, tq, layer):
    b, t_len, _ = x.shape
    n_hgrn = lb_raw.shape[0]
    st_blk = pl.BlockSpec((nb, HG_HEADS, HG_EXPAND, HG_EXPAND), lambda bi, ti: (bi, 0, 0, 0))
    return pl.pallas_call(
        functools.partial(_hgrn_kernel, nb=nb, tq=tq, layer=layer),
        grid=(b // nb, t_len // tq),
        in_specs=[
            pl.BlockSpec((nb, tq, D_MODEL), lambda bi, ti: (bi, ti, 0)),
            _resident((1, D_MODEL)),
            _resident((D_MODEL, HG_IN)),
            _resident((n_hgrn, HG_F)),
            _resident((1, HG_EXPAND)),
            _resident((D_MODEL, D_MODEL)),
            _resident((CHUNK, 3 * CHUNK)),
            st_blk,
        ],
        out_specs=[
            pl.BlockSpec((nb, tq, D_MODEL), lambda bi, ti: (bi, ti, 0)),
            st_blk,
        ],
        out_shape=[
            jax.ShapeDtypeStruct((b, t_len, D_MODEL), F32),
            jax.ShapeDtypeStruct((b, HG_HEADS, HG_EXPAND, HG_EXPAND), F32),
        ],
        scratch_shapes=[
            pltpu.VMEM((nb * tq, HG_IN), F32),
            pltpu.VMEM((nb * tq, D_MODEL), BF16),
        ],
        compiler_params=pltpu.CompilerParams(
            dimension_semantics=("arbitrary", "arbitrary"), vmem_limit_bytes=VMEM_LIMIT_BYTES),
        name="hgrn_layer",
    )(x, g, w_in, lb_raw, out_norm, w_o, seg, s0)


def _pipeline_fillers(a_pieces, ups, downs, tail, n_slots):
    n_ff = len(ups)
    seq = []
    for j in range(max(len(a_pieces), n_ff) + 1):
        if j < len(a_pieces):
            seq.append(a_pieces[j])
        if j < n_ff:
            seq.append(ups[j])
        if 1 <= j <= n_ff:
            seq.append(downs[j - 1])
    tail = [seq.pop()] + tail
    gaps = n_slots - 1
    fillers = [seq[g * len(seq) // gaps:(g + 1) * len(seq) // gaps] for g in range(gaps)]
    return fillers + [tail]


def _fused_swa_kernel(sink_ref, xa_ref, xb_ref, g_ref, wqkv_ref, wo_ref, cos_ref, sa_ref, sb_ref,
                      gm_ref, wu_ref, wd_ref, fg_ref, out_ref, nk_ref, nv_ref,
                      *scr, tq, nt, nblk, final):
    set0, set1 = scr[0:5], scr[5:10]
    y1_0, y1_1, o_scr, ck_scr, cv_scr = scr[10:15]
    s = pl.program_id(0)
    t_a = jnp.minimum(s, nblk - 1) % nt
    t_b = jnp.clip(s - 1, 0, nblk - 1) % nt

    @pl.when(s == 0)
    def _():
        for ref in set1 + (y1_0,):
            ref[...] = jnp.zeros_like(ref)

    @pl.when(t_a == 0)
    def _():
        ck_scr[...] = jnp.zeros_like(ck_scr)
        cv_scr[...] = jnp.zeros_like(cv_scr)

    def step(wset, rset, y1w, y1r):
        a_pieces = _swa_project_pieces(lambda: xa_ref[...], g_ref, wqkv_ref,
                                       (cos_ref, sa_ref, sb_ref), ck_scr, cv_scr, wset,
                                       nb=1, tq=tq)
        slots = _swa_core_slots(rset, o_scr, sink_ref, t_b * tq - WINDOW, nb=1, tq=tq)

        def store(v):
            out_ref[...] = v

        ups, downs = _mlp_pieces(lambda: y1r[...], gm_ref, wu_ref, wd_ref, fg_ref, store,
                                 final=final)

        def out_proj():
            y1w[...] = xb_ref[...] + _dot(o_scr[...], wo_ref[...])

        _interleave(slots, _pipeline_fillers(a_pieces, ups, downs, [out_proj], len(slots)))

    @pl.when(s % 2 == 0)
    def _():
        step(set0, set1, y1_1, y1_0)

    @pl.when(s % 2 == 1)
    def _():
        step(set1, set0, y1_0, y1_1)

    @pl.when(t_a == nt - 1)
    def _():
        nk_ref[...] = ck_scr[...]
        nv_ref[...] = cv_scr[...]


def _fused_hgrn_kernel(xa_ref, xb_ref, g_ref, win_ref, lbraw_ref, onorm_ref, wo_ref, seg_ref,
                       gm_ref, wu_ref, wd_ref, fg_ref, out_ref, st_ref,
                       z0, z1, y1_0, y1_1, o_scr, st_scr, *, tq, nt, nblk, layer, final):
    s = pl.program_id(0)
    t_b = jnp.clip(s - 1, 0, nblk - 1) % nt

    @pl.when(s == 0)
    def _():
        z1[...] = jnp.zeros_like(z1)
        y1_0[...] = jnp.zeros_like(y1_0)

    @pl.when(t_b == 0)
    def _():
        st_scr[...] = jnp.zeros_like(st_scr)

    def step(zw, zr, y1w, y1r):
        a_pieces = _hgrn_project_pieces(lambda: xa_ref[...], g_ref, win_ref, zw)
        slots = _hgrn_core_slots(zr, st_scr, o_scr, lbraw_ref, onorm_ref, seg_ref,
                                 nb=1, tq=tq, layer=layer)

        def store(v):
            out_ref[...] = v

        ups, downs = _mlp_pieces(lambda: y1r[...], gm_ref, wu_ref, wd_ref, fg_ref, store,
                                 final=final)

        def out_proj():
            y1w[...] = xb_ref[...] + _dot(o_scr[...], wo_ref[...])

        _interleave(slots, _pipeline_fillers(a_pieces, ups, downs, [out_proj], len(slots)))

    @pl.when(s % 2 == 0)
    def _():
        step(z0, z1, y1_1, y1_0)

    @pl.when(s % 2 == 1)
    def _():
        step(z1, z0, y1_0, y1_1)

    @pl.when((t_b == nt - 1) & (s >= 1) & (s <= nblk))
    def _():
        st_ref[...] = st_scr[...]


def _fused_layer(kind, x, tq, mixer_args, mlp_args, *, layer, final):
    b, t_len, _ = x.shape
    nt = t_len // tq
    nblk = b * nt
    x2 = x.reshape(b * t_len, D_MODEL)

    def blk_a(s):
        return jnp.minimum(s, nblk - 1)

    def blk_b(s):
        return jnp.clip(s - 1, 0, nblk - 1)

    def blk_c(s):
        return jnp.clip(s - 2, 0, nblk - 1)

    xa_spec = pl.BlockSpec((tq, D_MODEL), lambda s: (blk_a(s), 0))
    xb_spec = pl.BlockSpec((tq, D_MODEL), lambda s: (blk_b(s), 0))
    out_spec = pl.BlockSpec((tq, D_MODEL), lambda s: (blk_c(s), 0))
    mlp_specs = [_resident((1, D_MODEL)), _resident((D_MODEL, D_FF)),
                 _resident((D_FF, D_MODEL)), _resident((1, D_MODEL))]
    y1_scr = [pltpu.VMEM((tq, D_MODEL), F32), pltpu.VMEM((tq, D_MODEL), F32)]
    o_scr = pltpu.VMEM((tq, D_MODEL), BF16)
    params = pltpu.CompilerParams(dimension_semantics=("arbitrary",),
                                  vmem_limit_bytes=VMEM_LIMIT_BYTES)
    y_shape = jax.ShapeDtypeStruct((b * t_len, D_MODEL), F32)

    if kind == "swa":
        g, w_qkv, w_o, sinks, (cos, sa, sb) = mixer_args
        tab = pl.BlockSpec((tq, LANES), lambda s: (blk_a(s) % nt, 0))
        cache_blk = pl.BlockSpec((1, WINDOW, KV_DIM), lambda s: (blk_a(s) // nt, 0, 0))
        cache_shape = jax.ShapeDtypeStruct((b, WINDOW, KV_DIM), F32)
        y, nk, nv = pl.pallas_call(
            functools.partial(_fused_swa_kernel, tq=tq, nt=nt, nblk=nblk, final=final),
            grid=(nblk + 2,),
            in_specs=[pl.BlockSpec(memory_space=pltpu.SMEM), xa_spec, xb_spec,
                      _resident((1, D_MODEL)), _resident((D_MODEL, Q_DIM + 2 * KV_DIM)),
                      _resident((Q_DIM, D_MODEL)), tab, tab, tab] + mlp_specs,
            out_specs=[out_spec, cache_blk, cache_blk],
            out_shape=[y_shape, cache_shape, cache_shape],
            scratch_shapes=(_swa_scratch(1, tq) + _swa_scratch(1, tq) + y1_scr + [o_scr]
                            + [pltpu.VMEM((1, WINDOW, KV_DIM), F32) for _ in range(2)]),
            compiler_params=params,
            name="swa_mlp_layer",
        )(sinks, x2, x2, g, w_qkv, w_o, cos, sa, sb, *mlp_args)
        return y.reshape(b, t_len, D_MODEL), (nk, nv)

    g, w_in, lb_raw, onorm, w_o, seg = mixer_args
    st_blk = pl.BlockSpec((1, HG_HEADS, HG_EXPAND, HG_EXPAND),
                          lambda s: (blk_b(s) // nt, 0, 0, 0))
    y, st = pl.pallas_call(
        functools.partial(_fused_hgrn_kernel, tq=tq, nt=nt, nblk=nblk, layer=layer, final=final),
        grid=(nblk + 2,),
        in_specs=[xa_spec, xb_spec, _resident((1, D_MODEL)), _resident((D_MODEL, HG_IN)),
                  _resident((lb_raw.shape[0], HG_F)), _resident((1, HG_EXPAND)),
                  _resident((D_MODEL, D_MODEL)), _resident((CHUNK, 3 * CHUNK))] + mlp_specs,
        out_specs=[out_spec, st_blk],
        out_shape=[y_shape, jax.ShapeDtypeStruct((b, HG_HEADS, HG_EXPAND, HG_EXPAND), F32)],
        scratch_shapes=([pltpu.VMEM((tq, HG_IN), F32), pltpu.VMEM((tq, HG_IN), F32)] + y1_scr
                        + [o_scr, pltpu.VMEM((1, HG_HEADS, HG_EXPAND, HG_EXPAND), F32)]),
        compiler_params=params,
        name="hgrn_mlp_layer",
    )(x2, x2, g, w_in, lb_raw, onorm, w_o, seg, *mlp_args)
    return y.reshape(b, t_len, D_MODEL), (st,)


def kernel(x_prompt, x_sample, cache_k, cache_v, state_s, mixer_norm, mlp_norm, attn_w_qkv,
           attn_w_o, attn_sinks, hgrn_w_in, hgrn_lb, hgrn_out_norm, hgrn_w_o, mlp_w_up,
           mlp_w_down, final_norm):
    bp, tp, _ = x_prompt.shape
    bs, ts, _ = x_sample.shape
    assert cache_k.shape[2] == WINDOW and ts % CHUNK == 0 and tp % CHUNK == 0

    w_qkv = attn_w_qkv.astype(BF16)
    w_ao = attn_w_o.astype(BF16)
    w_in = hgrn_w_in.astype(BF16)
    w_ho = hgrn_w_o.astype(BF16)
    w_up = mlp_w_up.astype(BF16)
    w_down = mlp_w_down.astype(BF16)
    lb_raw = hgrn_lb.astype(F32)
    seg = jnp.asarray(_segment_matrix(), BF16)
    final_g = final_norm.reshape(1, D_MODEL)

    rope_p = _rope_tables(jnp.arange(tp, dtype=F32))
    rope_s = _rope_tables(PAST_LEN + jnp.arange(ts, dtype=F32))
    tq_p = _row_block(tp, 256)
    assert tq_p >= WINDOW

    yp, ys = x_prompt, x_sample
    kp_l, vp_l, sp_l, ks_l, vs_l, ss_l = [], [], [], [], [], []
    for i in range(DEPTH):
        j = i // 2
        g = mixer_norm[i].reshape(1, D_MODEL)
        gm = mlp_norm[i].reshape(1, D_MODEL)
        final = i == DEPTH - 1
        mlp_args = (gm, w_up[i], w_down[i], final_g)
        if i % 2 == 0:
            yp, (kp, vp) = _fused_layer("swa", yp, tq_p,
                                        (g, w_qkv[j], w_ao[j], attn_sinks[j], rope_p),
                                        mlp_args, layer=j, final=final)
            ys, kn, vn = _attn_layer(ys, g, w_qkv[j], w_ao[j], attn_sinks[j], rope_s,
                                     cache_k[j].reshape(bs, WINDOW, KV_DIM),
                                     cache_v[j].reshape(bs, WINDOW, KV_DIM),
                                     nb=1, tq=ts, pos0=PAST_LEN)
            kp_l.append(kp.reshape(bp, WINDOW, N_KV, HEAD_DIM))
            vp_l.append(vp.reshape(bp, WINDOW, N_KV, HEAD_DIM))
            ks_l.append(kn.reshape(bs, WINDOW, N_KV, HEAD_DIM))
            vs_l.append(vn.reshape(bs, WINDOW, N_KV, HEAD_DIM))
        else:
            onorm = hgrn_out_norm[j].reshape(1, HG_EXPAND)
            yp, (sp,) = _fused_layer("hgrn", yp, tq_p,
                                     (g, w_in[j], lb_raw, onorm, w_ho[j], seg),
                                     mlp_args, layer=j, final=final)
            ys, sn = _hgrn_layer(ys, g, w_in[j], lb_raw, onorm, w_ho[j], seg, state_s[j],
                                 nb=1, tq=ts, layer=j)
            sp_l.append(sp)
            ss_l.append(sn)
        ys = _mlp(ys.reshape(bs * ts, D_MODEL), gm, w_up[i], w_down[i], final_g,
                  final).reshape(bs, ts, D_MODEL)
    return (yp, ys, jnp.stack(kp_l), jnp.stack(vp_l), jnp.stack(sp_l),
            jnp.stack(ks_l), jnp.stack(vs_l), jnp.stack(ss_l))
```

```python
import functools
import math

import jax
import jax.numpy as jnp
import numpy as np
from jax import lax
from jax.experimental import pallas as pl
from jax.experimental.pallas import tpu as pltpu

D_MODEL = 1024
DEPTH = 4
PAST_LEN = 2048
CHUNK = 64
N_HEADS = 16
N_KV = 4
HEAD_DIM = 64
GROUP = N_HEADS // N_KV
ROT_DIM = HEAD_DIM // 4
ROPE_THETA = 500000.0
WINDOW = 128
Q_DIM = N_HEADS * HEAD_DIM
KV_DIM = N_KV * HEAD_DIM
HG_EXPAND = 128
HG_HEADS = D_MODEL // HG_EXPAND
HG_F = HG_HEADS * HG_EXPAND
HG_IN = 2 * HG_F + 2 * D_MODEL
D_FF = 4 * D_MODEL
EPS = 1e-5

LANES = 128
VMEM_LIMIT_BYTES = 56 * 1024 * 1024

KEYS = WINDOW + CHUNK
N_LEVELS = 6
assert 1 << N_LEVELS == CHUNK
N_SLAB = Q_DIM // LANES
FF_BLOCK = 1024
PROJ_BLOCK = 1024
Q_BLOCK = 512
LOG2_E = math.log2(math.e)

F32 = jnp.float32
BF16 = jnp.bfloat16


def _rms(x, g):
    ms = jnp.mean(x * x, axis=-1, keepdims=True)
    return x * lax.rsqrt(ms + EPS) * g


def _dot(a, b):
    return jnp.dot(a, b, preferred_element_type=F32)


def _dot_nt(a, b):
    return lax.dot_general(a, b, (((1,), (1,)), ((), ())), preferred_element_type=F32)


def _dot_tn(a, b):
    return lax.dot_general(a, b, (((0,), (0,)), ((), ())), preferred_element_type=F32)


def _block_diag(a):
    w = a.shape[1] // 2
    z = jnp.zeros_like(a[:, :w])
    return jnp.concatenate([jnp.concatenate([a[:, :w], z], axis=1),
                            jnp.concatenate([z, a[:, w:]], axis=1)], axis=0)


def _resident(shape):
    nd = len(shape)
    return pl.BlockSpec(shape, lambda *_: (0,) * nd, pipeline_mode=pl.Buffered(1))


def _row_block(total, want):
    blk = min(total, want)
    while total % blk:
        blk -= CHUNK
    return blk


def _run(pieces):
    for piece in pieces:
        piece()


def _interleave(slots, fillers):
    for k, slot in enumerate(slots):
        slot()
        if k < len(fillers):
            _run(fillers[k])


def _mlp_pieces(x_fn, g_ref, wu_ref, wd_ref, fg_ref, out_fn, *, final):
    st = {}

    def up(j):
        if j == 0:
            st["x"] = x_fn()
            st["h"] = _rms(st["x"], g_ref[...]).astype(BF16)
            st["acc"] = st["x"]
        u = _dot(st["h"], wu_ref[:, j * FF_BLOCK:(j + 1) * FF_BLOCK])
        st[j] = jnp.square(jnp.maximum(u, 0.0)).astype(BF16)

    def down(j):
        st["acc"] = st["acc"] + _dot(st.pop(j), wd_ref[j * FF_BLOCK:(j + 1) * FF_BLOCK, :])
        if j == D_FF // FF_BLOCK - 1:
            acc = st["acc"]
            out_fn(_rms(acc, fg_ref[...]) if final else acc)

    n = D_FF // FF_BLOCK
    return ([functools.partial(up, j) for j in range(n)],
            [functools.partial(down, j) for j in range(n)])


def _mlp_kernel(x_ref, g_ref, wu_ref, wd_ref, fg_ref, o_ref, *, final):
    def store(v):
        o_ref[...] = v

    ups, downs = _mlp_pieces(lambda: x_ref[...], g_ref, wu_ref, wd_ref, fg_ref, store, final=final)
    for up, down in zip(ups, downs):
        up()
        down()


def _mlp(x, g, w_up, w_down, final_g, final):
    m = x.shape[0]
    tm = _row_block(m, 512)
    return pl.pallas_call(
        functools.partial(_mlp_kernel, final=final),
        grid=(m // tm,),
        in_specs=[
            pl.BlockSpec((tm, D_MODEL), lambda i: (i, 0)),
            _resident((1, D_MODEL)),
            _resident((D_MODEL, D_FF)),
            _resident((D_FF, D_MODEL)),
            _resident((1, D_MODEL)),
        ],
        out_specs=pl.BlockSpec((tm, D_MODEL), lambda i: (i, 0)),
        out_shape=jax.ShapeDtypeStruct((m, D_MODEL), F32),
        compiler_params=pltpu.CompilerParams(
            dimension_semantics=("arbitrary",), vmem_limit_bytes=VMEM_LIMIT_BYTES),
        name="mlp",
    )(x, g, w_up, w_down, final_g)


def _swa_project_pieces(x_fn, g_ref, wqkv_ref, tabs, ck_ref, cv_ref, bufs, *, nb, tq):
    q_scr, ka_scr, kb_scr, va_scr, vb_scr = bufs
    cos_ref, sa_ref, sb_ref = tabs
    st = {}

    def hidden():
        if "h" not in st:
            st["h"] = _rms(x_fn(), g_ref[...]).astype(BF16)
        return st["h"]

    def rope(xs):
        return (xs * cos_ref[...] + pltpu.roll(xs, ROT_DIM // 2, 1) * sa_ref[...]
                + pltpu.roll(xs, LANES - ROT_DIM // 2, 1) * sb_ref[...])

    scale = HEAD_DIM ** -0.5

    def q_piece(j):
        qb = _dot(hidden(), wqkv_ref[:, j * Q_BLOCK:(j + 1) * Q_BLOCK])
        for i in range(nb):
            rs = slice(i * tq, (i + 1) * tq)
            for s in range(Q_BLOCK // LANES):
                qs = rope(qb[rs, s * LANES:(s + 1) * LANES]) * scale
                c0 = j * Q_BLOCK + s * LANES
                q_scr[rs, c0:c0 + LANES] = qs.astype(BF16)

    def kv_piece():
        kv = _dot(hidden(), wqkv_ref[:, Q_DIM:Q_DIM + 2 * KV_DIM])
        low = lax.broadcasted_iota(jnp.int32, (1, LANES), 1) < HEAD_DIM
        for i in range(nb):
            rs = slice(i * tq, (i + 1) * tq)
            for s in range(KV_DIM // LANES):
                ls = slice(s * LANES, (s + 1) * LANES)
                knew = rope(kv[rs, ls])
                vnew = kv[rs, KV_DIM + s * LANES:KV_DIM + (s + 1) * LANES]
                kall = jnp.concatenate([ck_ref[i, :, ls], knew], axis=0)
                vall = jnp.concatenate([cv_ref[i, :, ls], vnew], axis=0)
                ck_ref[i, :, ls] = kall[tq:tq + WINDOW]
                cv_ref[i, :, ls] = vall[tq:tq + WINDOW]
                krot = pltpu.roll(kall, HEAD_DIM, 1)
                vrot = pltpu.roll(vall, HEAD_DIM, 1)
                zero = jnp.zeros_like(kall)
                ka_scr[i, 2 * s] = jnp.where(low, kall, zero).astype(BF16)
                kb_scr[i, 2 * s] = jnp.where(low, zero, krot).astype(BF16)
                ka_scr[i, 2 * s + 1] = jnp.where(low, krot, zero).astype(BF16)
                kb_scr[i, 2 * s + 1] = jnp.where(low, zero, kall).astype(BF16)
                va_scr[i, 2 * s] = jnp.where(low, vall, zero).astype(BF16)
                vb_scr[i, 2 * s] = jnp.where(low, zero, vrot).astype(BF16)
                va_scr[i, 2 * s + 1] = jnp.where(low, vrot, zero).astype(BF16)
                vb_scr[i, 2 * s + 1] = jnp.where(low, zero, vall).astype(BF16)

    return [functools.partial(q_piece, j) for j in range(Q_DIM // Q_BLOCK)] + [kv_piece]


def _swa_core_slots(bufs, o_scr, sink_ref, key_pos0, *, nb, tq):
    q_scr, ka_scr, kb_scr, va_scr, vb_scr = bufs
    nch = tq // CHUNK
    work = [(i, c) for i in range(nb) for c in range(nch)]
    key_iota = lax.broadcasted_iota(jnp.int32, (1, KEYS), 1)
    st = {}

    def softmax_sink(s, valid, sink):
        s = jnp.where(valid, s, -jnp.inf)
        m = jnp.maximum(jnp.max(s, axis=-1, keepdims=True), sink)
        e = jnp.exp(s - m)
        den = jnp.sum(e, axis=-1, keepdims=True) + jnp.exp(sink - m)
        return (e / den).astype(BF16)

    spk = GROUP // 2

    def scores(w):
        i, c = work[w]
        r0, k0 = i * tq + c * CHUNK, c * CHUNK
        out = []
        for hh in range(N_KV):
            qs = jnp.concatenate(
                [q_scr[r0:r0 + CHUNK, s * LANES:(s + 1) * LANES]
                 for s in range(hh * spk, (hh + 1) * spk)], axis=0)
            out.append((_dot_nt(qs, ka_scr[i, hh, k0:k0 + KEYS, :]),
                        _dot_nt(qs, kb_scr[i, hh, k0:k0 + KEYS, :])))
        st[w] = out

    def attend(w):
        i, c = work[w]
        r0, k0 = i * tq + c * CHUNK, c * CHUNK
        sc = st.pop(w)
        valid = (key_pos0 + k0 + key_iota) >= 0
        for hh in range(N_KV):
            pa, pb = [], []
            for j in range(spk):
                slab = hh * spk + j
                rs = slice(j * CHUNK, (j + 1) * CHUNK)
                pa.append(softmax_sink(sc[hh][0][rs], valid, sink_ref[2 * slab]))
                pb.append(softmax_sink(sc[hh][1][rs], valid, sink_ref[2 * slab + 1]))
            o = (_dot(jnp.concatenate(pa, axis=0), va_scr[i, hh, k0:k0 + KEYS, :])
                 + _dot(jnp.concatenate(pb, axis=0), vb_scr[i, hh, k0:k0 + KEYS, :]))
            for j in range(spk):
                slab = hh * spk + j
                o_scr[r0:r0 + CHUNK, slab * LANES:(slab + 1) * LANES] = (
                    o[j * CHUNK:(j + 1) * CHUNK].astype(BF16))

    def slot(w):
        if w < len(work):
            scores(w)
        if w > 0:
            attend(w - 1)

    return [functools.partial(slot, w) for w in range(len(work) + 1)]


def _attn_kernel(sink_ref, x_ref, g_ref, wqkv_ref, wo_ref, cos_ref, sa_ref, sb_ref,
                 ck0_ref, cv0_ref, y_ref, nk_ref, nv_ref,
                 q_scr, ka_scr, kb_scr, va_scr, vb_scr, o_scr, *, nb, tq, pos0):
    t = pl.program_id(1)

    @pl.when(t == 0)
    def _():
        nk_ref[...] = ck0_ref[...]
        nv_ref[...] = cv0_ref[...]

    bufs = (q_scr, ka_scr, kb_scr, va_scr, vb_scr)
    x = x_ref[...].reshape(nb * tq, D_MODEL)
    _run(_swa_project_pieces(lambda: x, g_ref, wqkv_ref, (cos_ref, sa_ref, sb_ref),
                             nk_ref, nv_ref, bufs, nb=nb, tq=tq))
    _run(_swa_core_slots(bufs, o_scr, sink_ref, pos0 + t * tq - WINDOW, nb=nb, tq=tq))
    y = x + _dot(o_scr[...], wo_ref[...])
    y_ref[...] = y.reshape(nb, tq, D_MODEL)


def _swa_scratch(nb, tq):
    kvlen = WINDOW + tq
    return [pltpu.VMEM((nb * tq, Q_DIM), BF16)] + [
        pltpu.VMEM((nb, N_KV, kvlen, LANES), BF16) for _ in range(4)]


def _attn_layer(x, g, w_qkv, w_o, sinks, rope_tabs, cache_k, cache_v, *, nb, tq, pos0):
    b, t_len, _ = x.shape
    cos, sa, sb = rope_tabs
    cache_blk = pl.BlockSpec((nb, WINDOW, KV_DIM), lambda bi, ti: (bi, 0, 0))
    tab_blk = pl.BlockSpec((tq, LANES), lambda bi, ti: (ti, 0))
    return pl.pallas_call(
        functools.partial(_attn_kernel, nb=nb, tq=tq, pos0=pos0),
        grid=(b // nb, t_len // tq),
        in_specs=[
            pl.BlockSpec(memory_space=pltpu.SMEM),
            pl.BlockSpec((nb, tq, D_MODEL), lambda bi, ti: (bi, ti, 0)),
            _resident((1, D_MODEL)),
            _resident((D_MODEL, Q_DIM + 2 * KV_DIM)),
            _resident((Q_DIM, D_MODEL)),
            tab_blk, tab_blk, tab_blk,
            cache_blk, cache_blk,
        ],
        out_specs=[
            pl.BlockSpec((nb, tq, D_MODEL), lambda bi, ti: (bi, ti, 0)),
            cache_blk, cache_blk,
        ],
        out_shape=[
            jax.ShapeDtypeStruct((b, t_len, D_MODEL), F32),
            jax.ShapeDtypeStruct((b, WINDOW, KV_DIM), F32),
            jax.ShapeDtypeStruct((b, WINDOW, KV_DIM), F32),
        ],
        scratch_shapes=_swa_scratch(nb, tq) + [pltpu.VMEM((nb * tq, Q_DIM), BF16)],
        compiler_params=pltpu.CompilerParams(
            dimension_semantics=("arbitrary", "arbitrary"), vmem_limit_bytes=VMEM_LIMIT_BYTES),
        name="swa_layer",
    )(sinks, x, g, w_qkv, w_o, cos, sa, sb, cache_k, cache_v)


def _rope_tables(pos):
    half = ROT_DIM // 2
    inv_freq = ROPE_THETA ** (-(jnp.arange(half, dtype=F32) * 2.0) / ROT_DIM)
    ang = pos[:, None] * inv_freq[None, :]
    cos, sin = jnp.cos(ang), jnp.sin(ang)
    n = pos.shape[0]
    rest = HEAD_DIM - ROT_DIM
    c64 = jnp.concatenate([cos, cos, jnp.ones((n, rest), F32)], axis=1)
    sa64 = jnp.concatenate([jnp.zeros((n, half), F32), sin, jnp.zeros((n, rest), F32)], axis=1)
    sb64 = jnp.concatenate([-sin, jnp.zeros((n, half + rest), F32)], axis=1)
    rep = LANES // HEAD_DIM
    return tuple(jnp.tile(a, (1, rep)) for a in (c64, sa64, sb64))


def _segment_matrix():
    tril = np.tril(np.ones((CHUNK, CHUNK), np.float32))
    return np.concatenate([tril, tril, tril], axis=1)


def _hgrn_project_pieces(x_fn, g_ref, win_ref, z_scr):
    st = {}

    def piece(j):
        if "h" not in st:
            st["h"] = _rms(x_fn(), g_ref[...]).astype(BF16)
        cs_ = slice(j * PROJ_BLOCK, (j + 1) * PROJ_BLOCK)
        z_scr[:, cs_] = _dot(st["h"], win_ref[:, cs_])

    return [functools.partial(piece, j) for j in range(HG_IN // PROJ_BLOCK)]


def _hgrn_core_slots(z_scr, st_ref, o_scr, lbraw_ref, onorm_ref, seg_ref, *, nb, tq, layer):
    nch = tq // CHUNK
    work = [(i, c) for i in range(nb) for c in range(nch)]
    st = {}

    def consts():
        if "lb" in st:
            return
        lbraw = lbraw_ref[...]
        e = jnp.exp(lbraw - jnp.max(lbraw, axis=0, keepdims=True))
        sm = e / jnp.sum(e, axis=0, keepdims=True)
        cs0 = sm[0:1]
        cs = cs0
        for r in range(1, layer + 1):
            cs = cs + sm[r:r + 1]
        st["lb"] = cs - cs0
        st["one_m_lb"] = 1.0 - st["lb"]
        ti = lax.broadcasted_iota(jnp.int32, (CHUNK, 2 * CHUNK), 0)
        si = lax.broadcasted_iota(jnp.int32, (CHUNK, 2 * CHUNK), 1) & (CHUNK - 1)
        masks = [ti == si]
        for l in range(N_LEVELS):
            masks.append(((ti >> (l + 1)) == (si >> (l + 1)))
                         & (((ti >> l) & 1) == 1) & (((si >> l) & 1) == 0))
        st["masks"] = masks

    def intra(w):
        consts()
        i, c = work[w]
        lb, one_m_lb, masks = st["lb"], st["one_m_lb"], st["masks"]
        r0 = i * tq + c * CHUNK
        zq = z_scr[r0:r0 + CHUNK, 0:HG_F]
        zf = z_scr[r0:r0 + CHUNK, HG_F:2 * HG_F]
        zi = z_scr[r0:r0 + CHUNK, 2 * HG_F:2 * HG_F + D_MODEL]
        zg = z_scr[r0:r0 + CHUNK, 2 * HG_F + D_MODEL:]
        q = zq * (1.0 / (1.0 + jnp.exp(-zq)))
        et = jnp.exp(-jnp.abs(zf))
        rt = 1.0 / (1.0 + et)
        nonneg = zf >= 0.0
        sig_pos = jnp.where(nonneg, rt, et * rt)
        sig_neg = jnp.where(nonneg, et * rt, rt)
        fgate = lb + one_m_lb * sig_pos
        kk = one_m_lb * sig_neg
        gate = zg * (1.0 / (1.0 + jnp.exp(-zg)))
        l2f = jnp.log(fgate) * LOG2_E
        hi = l2f.astype(BF16)
        r1 = l2f - hi.astype(F32)
        mid = r1.astype(BF16)
        lo = (r1 - mid.astype(F32)).astype(BF16)
        b = _dot(seg_ref[...], jnp.concatenate([hi, mid, lo], axis=0))
        row = lax.broadcasted_iota(jnp.int32, (CHUNK, 1), 0)
        elev = [jnp.where((row & 1) == 1, fgate, 1.0)]
        for l in range(1, N_LEVELS):
            n = 1 << l
            if 2 * n >= 8:
                bref = jnp.concatenate(
                    [jnp.broadcast_to(b[blk + n - 1:blk + n, :], (2 * n, HG_F))
                     for blk in range(0, CHUNK, 2 * n)], axis=0)
            else:
                sub = lax.broadcasted_iota(jnp.int32, (8, 1), 0)
                bref = jnp.concatenate(
                    [jnp.where(((sub >> (l + 1)) & 1) == 0,
                               jnp.broadcast_to(b[g + n - 1:g + n, :], (8, HG_F)),
                               jnp.broadcast_to(b[g + 3 * n - 1:g + 3 * n, :], (8, HG_F)))
                     for g in range(0, CHUNK, 8)], axis=0)
            sgn = jnp.where(((row >> l) & 1) == 1, 1.0, -1.0)
            elev.append(jnp.exp2((b - bref) * sgn))
        eb_all = jnp.exp2(b)
        er_all = jnp.exp2(b[CHUNK - 1:CHUNK, :] - b)
        pairs = []
        for p in range(HG_HEADS // 2):
            cs_ = slice(2 * p * HG_EXPAND, (2 * p + 2) * HG_EXPAND)
            qp, kp = q[:, cs_], kk[:, cs_]
            sc = jnp.where(masks[0], _dot_nt(qp.astype(BF16), _block_diag(kp.astype(BF16))), 0.0)
            for l in range(N_LEVELS):
                el = elev[l][:, cs_]
                pr = _dot_nt((qp * el).astype(BF16), _block_diag((kp * el).astype(BF16)))
                sc = jnp.where(masks[l + 1], pr, sc)
            eb, er = eb_all[:, cs_], er_all[:, cs_]
            eb_last = jnp.broadcast_to(eb[CHUNK - 1:CHUNK, :], (8, 2 * HG_EXPAND))
            pairs.append(dict(
                sc=sc.astype(BF16), qe=(qp * eb).astype(BF16), ke=(kp * er).astype(BF16),
                v=zi[:, cs_].astype(BF16), gate=gate[:, cs_],
                eb_col=jnp.transpose(eb_last)[:, 0:1]))
        st[w] = pairs

    def inter(w):
        i, c = work[w]
        pairs = st.pop(w)
        onorm = onorm_ref[...]
        r0 = i * tq + c * CHUNK
        states = [st_ref[i, hh] for hh in range(HG_HEADS)]
        o_list = []
        for p, pd in enumerate(pairs):
            sbd = _block_diag(jnp.concatenate(
                [states[2 * p].astype(BF16), states[2 * p + 1].astype(BF16)], axis=1))
            o_list.append(_dot(pd["sc"], _block_diag(pd["v"])) + _dot(pd["qe"], sbd))
        new_states = []
        for hh in range(HG_HEADS):
            pd, cs_ = pairs[hh // 2], slice((hh % 2) * HG_EXPAND, (hh % 2 + 1) * HG_EXPAND)
            new_states.append(pd["eb_col"][cs_] * states[hh]
                              + _dot_tn(pd["ke"][:, cs_], pd["v"][:, cs_]))
        for hh in range(HG_HEADS):
            cs_ = slice((hh % 2) * HG_EXPAND, (hh % 2 + 1) * HG_EXPAND)
            o = o_list[hh // 2][:, cs_]
            on = o * lax.rsqrt(jnp.mean(o * o, axis=-1, keepdims=True) + EPS) * onorm
            st_ref[i, hh] = new_states[hh]
            o_scr[r0:r0 + CHUNK, hh * HG_EXPAND:(hh + 1) * HG_EXPAND] = (
                on * pairs[hh // 2]["gate"][:, cs_]).astype(BF16)

    def slot(w):
        if w < len(work):
            intra(w)
        if w > 0:
            inter(w - 1)

    return [functools.partial(slot, w) for w in range(len(work) + 1)]


def _hgrn_kernel(x_ref, g_ref, win_ref, lbraw_ref, onorm_ref, wo_ref, seg_ref, s0_ref,
                 y_ref, st_ref, z_scr, o_scr, *, nb, tq, layer):
    t = pl.program_id(1)

    @pl.when(t == 0)
    def _():
        st_ref[...] = s0_ref[...]

    x = x_ref[...].reshape(nb * tq, D_MODEL)
    _run(_hgrn_project_pieces(lambda: x, g_ref, win_ref, z_scr))
    _run(_hgrn_core_slots(z_scr, st_ref, o_scr, lbraw_ref, onorm_ref, seg_ref,
                          nb=nb, tq=tq, layer=layer))
    y = x + _dot(o_scr[...], wo_ref[...])
    y_ref[...] = y.reshape(nb, tq, D_MODEL)


def _hgrn_layer(x, g, w_in, lb_raw, out_norm, w_o, seg, s0, *, nb, tq, layer):
    b, t_len, _ = x.shape
    n_hgrn = lb_raw.shape[0]
    st_blk = pl.BlockSpec((nb, HG_HEADS, HG_EXPAND, HG_EXPAND), lambda bi, ti: (bi, 0, 0, 0))
    return pl.pallas_call(
        functools.partial(_hgrn_kernel, nb=nb, tq=tq, layer=layer),
        grid=(b // nb, t_len // tq),
        in_specs=[
            pl.BlockSpec((nb, tq, D_MODEL), lambda bi, ti: (bi, ti, 0)),
            _resident((1, D_MODEL)),
            _resident((D_MODEL, HG_IN)),
            _resident((n_hgrn, HG_F)),
            _resident((1, HG_EXPAND)),
            _resident((D_MODEL, D_MODEL)),
            _resident((CHUNK, 3 * CHUNK)),
            st_blk,
        ],
        out_specs=[
            pl.BlockSpec((nb, tq, D_MODEL), lambda bi, ti: (bi, ti, 0)),
            st_blk,
        ],
        out_shape=[
            jax.ShapeDtypeStruct((b, t_len, D_MODEL), F32),
            jax.ShapeDtypeStruct((b, HG_HEADS, HG_EXPAND, HG_EXPAND), F32),
        ],
        scratch_shapes=[
            pltpu.VMEM((nb * tq, HG_IN), F32),
            pltpu.VMEM((nb * tq, D_MODEL), BF16),
        ],
        compiler_params=pltpu.CompilerParams(
            dimension_semantics=("arbitrary", "arbitrary"), vmem_limit_bytes=VMEM_LIMIT_BYTES),
        name="hgrn_layer",
    )(x, g, w_in, lb_raw, out_norm, w_o, seg, s0)


def _pipeline_fillers(a_pieces, ups, downs, tail, n_slots):
    n_ff = len(ups)
    seq = []
    for j in range(max(len(a_pieces), n_ff) + 1):
        if j < len(a_pieces):
            seq.append(a_pieces[j])
        if j < n_ff:
            seq.append(ups[j])
        if 1 <= j <= n_ff:
            seq.append(downs[j - 1])
    tail = [seq.pop()] + tail
    gaps = n_slots - 1
    fillers = [seq[g * len(seq) // gaps:(g + 1) * len(seq) // gaps] for g in range(gaps)]
    return fillers + [tail]


def _fused_swa_kernel(sink_ref, xa_ref, xb_ref, g_ref, wqkv_ref, wo_ref, cos_ref, sa_ref, sb_ref,
                      gm_ref, wu_ref, wd_ref, fg_ref, out_ref, nk_ref, nv_ref,
                      *scr, tq, nt, nblk, final):
    set0, set1 = scr[0:5], scr[5:10]
    y1_0, y1_1, o_scr, ck_scr, cv_scr = scr[10:15]
    s = pl.program_id(0)
    t_a = jnp.minimum(s, nblk - 1) % nt
    t_b = jnp.clip(s - 1, 0, nblk - 1) % nt

    @pl.when(s == 0)
    def _():
        for ref in set1 + (y1_0,):
            ref[...] = jnp.zeros_like(ref)

    @pl.when(t_a == 0)
    def _():
        ck_scr[...] = jnp.zeros_like(ck_scr)
        cv_scr[...] = jnp.zeros_like(cv_scr)

    def step(wset, rset, y1w, y1r):
        a_pieces = _swa_project_pieces(lambda: xa_ref[...], g_ref, wqkv_ref,
                                       (cos_ref, sa_ref, sb_ref), ck_scr, cv_scr, wset,
                                       nb=1, tq=tq)
        slots = _swa_core_slots(rset, o_scr, sink_ref, t_b * tq - WINDOW, nb=1, tq=tq)

        def store(v):
            out_ref[...] = v

        ups, downs = _mlp_pieces(lambda: y1r[...], gm_ref, wu_ref, wd_ref, fg_ref, store,
                                 final=final)

        def out_proj():
            y1w[...] = xb_ref[...] + _dot(o_scr[...], wo_ref[...])

        _interleave(slots, _pipeline_fillers(a_pieces, ups, downs, [out_proj], len(slots)))

    @pl.when(s % 2 == 0)
    def _():
        step(set0, set1, y1_1, y1_0)

    @pl.when(s % 2 == 1)
    def _():
        step(set1, set0, y1_0, y1_1)

    @pl.when(t_a == nt - 1)
    def _():
        nk_ref[...] = ck_scr[...]
        nv_ref[...] = cv_scr[...]


def _fused_hgrn_kernel(xa_ref, xb_ref, g_ref, win_ref, lbraw_ref, onorm_ref, wo_ref, seg_ref,
                       gm_ref, wu_ref, wd_ref, fg_ref, out_ref, st_ref,
                       z0, z1, y1_0, y1_1, o_scr, st_scr, *, tq, nt, nblk, layer, final):
    s = pl.program_id(0)
    t_b = jnp.clip(s - 1, 0, nblk - 1) % nt

    @pl.when(s == 0)
    def _():
        z1[...] = jnp.zeros_like(z1)
        y1_0[...] = jnp.zeros_like(y1_0)

    @pl.when(t_b == 0)
    def _():
        st_scr[...] = jnp.zeros_like(st_scr)

    def step(zw, zr, y1w, y1r):
        a_pieces = _hgrn_project_pieces(lambda: xa_ref[...], g_ref, win_ref, zw)
        slots = _hgrn_core_slots(zr, st_scr, o_scr, lbraw_ref, onorm_ref, seg_ref,
                                 nb=1, tq=tq, layer=layer)

        def store(v):
            out_ref[...] = v

        ups, downs = _mlp_pieces(lambda: y1r[...], gm_ref, wu_ref, wd_ref, fg_ref, store,
                                 final=final)

        def out_proj():
            y1w[...] = xb_ref[...] + _dot(o_scr[...], wo_ref[...])

        _interleave(slots, _pipeline_fillers(a_pieces, ups, downs, [out_proj], len(slots)))

    @pl.when(s % 2 == 0)
    def _():
        step(z0, z1, y1_1, y1_0)

    @pl.when(s % 2 == 1)
    def _():
        step(z1, z0, y1_0, y1_1)

    @pl.when((t_b == nt - 1) & (s >= 1) & (s <= nblk))
    def _():
        st_ref[...] = st_scr[...]


def _fused_layer(kind, x, tq, mixer_args, mlp_args, *, layer, final):
    b, t_len, _ = x.shape
    nt = t_len // tq
    nblk = b * nt
    x2 = x.reshape(b * t_len, D_MODEL)

    def blk_a(s):
        return jnp.minimum(s, nblk - 1)

    def blk_b(s):
        return jnp.clip(s - 1, 0, nblk - 1)

    def blk_c(s):
        return jnp.clip(s - 2, 0, nblk - 1)

    xa_spec = pl.BlockSpec((tq, D_MODEL), lambda s: (blk_a(s), 0))
    xb_spec = pl.BlockSpec((tq, D_MODEL), lambda s: (blk_b(s), 0))
    out_spec = pl.BlockSpec((tq, D_MODEL), lambda s: (blk_c(s), 0))
    mlp_specs = [_resident((1, D_MODEL)), _resident((D_MODEL, D_FF)),
                 _resident((D_FF, D_MODEL)), _resident((1, D_MODEL))]
    y1_scr = [pltpu.VMEM((tq, D_MODEL), F32), pltpu.VMEM((tq, D_MODEL), F32)]
    o_scr = pltpu.VMEM((tq, D_MODEL), BF16)
    params = pltpu.CompilerParams(dimension_semantics=("arbitrary",),
                                  vmem_limit_bytes=VMEM_LIMIT_BYTES)
    y_shape = jax.ShapeDtypeStruct((b * t_len, D_MODEL), F32)

    if kind == "swa":
        g, w_qkv, w_o, sinks, (cos, sa, sb) = mixer_args
        tab = pl.BlockSpec((tq, LANES), lambda s: (blk_a(s) % nt, 0))
        cache_blk = pl.BlockSpec((1, WINDOW, KV_DIM), lambda s: (blk_a(s) // nt, 0, 0))
        cache_shape = jax.ShapeDtypeStruct((b, WINDOW, KV_DIM), F32)
        y, nk, nv = pl.pallas_call(
            functools.partial(_fused_swa_kernel, tq=tq, nt=nt, nblk=nblk, final=final),
            grid=(nblk + 2,),
            in_specs=[pl.BlockSpec(memory_space=pltpu.SMEM), xa_spec, xb_spec,
                      _resident((1, D_MODEL)), _resident((D_MODEL, Q_DIM + 2 * KV_DIM)),
                      _resident((Q_DIM, D_MODEL)), tab, tab, tab] + mlp_specs,
            out_specs=[out_spec, cache_blk, cache_blk],
            out_shape=[y_shape, cache_shape, cache_shape],
            scratch_shapes=(_swa_scratch(1, tq) + _swa_scratch(1, tq) + y1_scr + [o_scr]
                            + [pltpu.VMEM((1, WINDOW, KV_DIM), F32) for _ in range(2)]),
            compiler_params=params,
            name="swa_mlp_layer",
        )(sinks, x2, x2, g, w_qkv, w_o, cos, sa, sb, *mlp_args)
        return y.reshape(b, t_len, D_MODEL), (nk, nv)

    g, w_in, lb_raw, onorm, w_o, seg = mixer_args
    st_blk = pl.BlockSpec((1, HG_HEADS, HG_EXPAND, HG_EXPAND),
                          lambda s: (blk_b(s) // nt, 0, 0, 0))
    y, st = pl.pallas_call(
        functools.partial(_fused_hgrn_kernel, tq=tq, nt=nt, nblk=nblk, layer=layer, final=final),
        grid=(nblk + 2,),
        in_specs=[xa_spec, xb_spec, _resident((1, D_MODEL)), _resident((D_MODEL, HG_IN)),
                  _resident((lb_raw.shape[0], HG_F)), _resident((1, HG_EXPAND)),
                  _resident((D_MODEL, D_MODEL)), _resident((CHUNK, 3 * CHUNK))] + mlp_specs,
        out_specs=[out_spec, st_blk],
        out_shape=[y_shape, jax.ShapeDtypeStruct((b, HG_HEADS, HG_EXPAND, HG_EXPAND), F32)],
        scratch_shapes=([pltpu.VMEM((tq, HG_IN), F32), pltpu.VMEM((tq, HG_IN), F32)] + y1_scr
                        + [o_scr, pltpu.VMEM((1, HG_HEADS, HG_EXPAND, HG_EXPAND), F32)]),
        compiler_params=params,
        name="hgrn_mlp_layer",
    )(x2, x2, g, w_in, lb_raw, onorm, w_o, seg, *mlp_args)
    return y.reshape(b, t_len, D_MODEL), (st,)


def kernel(x_prompt, x_sample, cache_k, cache_v, state_s, mixer_norm, mlp_norm, attn_w_qkv,
           attn_w_o, attn_sinks, hgrn_w_in, hgrn_lb, hgrn_out_norm, hgrn_w_o, mlp_w_up,
           mlp_w_down, final_norm):
    bp, tp, _ = x_prompt.shape
    bs, ts, _ = x_sample.shape
    assert cache_k.shape[2] == WINDOW and ts % CHUNK == 0 and tp % CHUNK == 0

    w_qkv = [attn_w_qkv[j].astype(BF16) for j in range(attn_w_qkv.shape[0])]
    w_ao = [attn_w_o[j].astype(BF16) for j in range(attn_w_o.shape[0])]
    w_in = [hgrn_w_in[j].astype(BF16) for j in range(hgrn_w_in.shape[0])]
    w_ho = [hgrn_w_o[j].astype(BF16) for j in range(hgrn_w_o.shape[0])]
    w_up = [mlp_w_up[i].astype(BF16) for i in range(DEPTH)]
    w_down = [mlp_w_down[i].astype(BF16) for i in range(DEPTH)]
    lb_raw = hgrn_lb.astype(F32)
    seg = jnp.asarray(_segment_matrix(), BF16)
    final_g = final_norm.reshape(1, D_MODEL)

    rope_p = _rope_tables(jnp.arange(tp, dtype=F32))
    rope_s = _rope_tables(PAST_LEN + jnp.arange(ts, dtype=F32))
    tq_p = _row_block(tp, 256)
    assert tq_p >= WINDOW

    yp, ys = x_prompt, x_sample
    kp_l, vp_l, sp_l, ks_l, vs_l, ss_l = [], [], [], [], [], []
    for i in range(DEPTH):
        j = i // 2
        g = mixer_norm[i].reshape(1, D_MODEL)
        gm = mlp_norm[i].reshape(1, D_MODEL)
        final = i == DEPTH - 1
        mlp_args = (gm, w_up[i], w_down[i], final_g)
        if i % 2 == 0:
            yp, (kp, vp) = _fused_layer("swa", yp, tq_p,
                                        (g, w_qkv[j], w_ao[j], attn_sinks[j], rope_p),
                                        mlp_args, layer=j, final=final)
            ys, kn, vn = _attn_layer(ys, g, w_qkv[j], w_ao[j], attn_sinks[j], rope_s,
                                     cache_k[j].reshape(bs, WINDOW, KV_DIM),
                                     cache_v[j].reshape(bs, WINDOW, KV_DIM),
                                     nb=bs, tq=ts, pos0=PAST_LEN)
            kp_l.append(kp.reshape(bp, WINDOW, N_KV, HEAD_DIM))
            vp_l.append(vp.reshape(bp, WINDOW, N_KV, HEAD_DIM))
            ks_l.append(kn.reshape(bs, WINDOW, N_KV, HEAD_DIM))
            vs_l.append(vn.reshape(bs, WINDOW, N_KV, HEAD_DIM))
        else:
            onorm = hgrn_out_norm[j].reshape(1, HG_EXPAND)
            yp, (sp,) = _fused_layer("hgrn", yp, tq_p,
                                     (g, w_in[j], lb_raw, onorm, w_ho[j], seg),
                                     mlp_args, layer=j, final=final)
            ys, sn = _hgrn_layer(ys, g, w_in[j], lb_raw, onorm, w_ho[j], seg, state_s[j],
                                 nb=bs, tq=ts, layer=j)
            sp_l.append(sp)
            ss_l.append(sn)
        ys = _mlp(ys.reshape(bs * ts, D_MODEL), gm, w_up[i], w_down[i], final_g,
                  final).reshape(bs, ts, D_MODEL)
    return (yp, ys, jnp.stack(kp_l), jnp.stack(vp_l), jnp.stack(sp_l),
            jnp.stack(ks_l), jnp.stack(vs_l), jnp.stack(ss_l))
```

```python
import functools
import math

import jax
import jax.numpy as jnp
import numpy as np
from jax import lax
from jax.experimental import pallas as pl
from jax.experimental.pallas import tpu as pltpu

D_MODEL = 1024
DEPTH = 4
PAST_LEN = 2048
CHUNK = 64
N_HEADS = 16
N_KV = 4
HEAD_DIM = 64
GROUP = N_HEADS // N_KV
ROT_DIM = HEAD_DIM // 4
ROPE_THETA = 500000.0
WINDOW = 128
Q_DIM = N_HEADS * HEAD_DIM
KV_DIM = N_KV * HEAD_DIM
HG_EXPAND = 128
HG_HEADS = D_MODEL // HG_EXPAND
HG_F = HG_HEADS * HG_EXPAND
HG_IN = 2 * HG_F + 2 * D_MODEL
D_FF = 4 * D_MODEL
EPS = 1e-5

LANES = 128
VMEM_LIMIT_BYTES = 56 * 1024 * 1024

KEYS = WINDOW + CHUNK
N_LEVELS = 6
assert 1 << N_LEVELS == CHUNK
N_SLAB = Q_DIM // LANES
FF_BLOCK = 1024
PROJ_BLOCK = 1024
Q_BLOCK = 512
LOG2_E = math.log2(math.e)

F32 = jnp.float32
BF16 = jnp.bfloat16


def _rms(x, g):
    ms = jnp.mean(x * x, axis=-1, keepdims=True)
    return x * lax.rsqrt(ms + EPS) * g


def _dot(a, b):
    return jnp.dot(a, b, preferred_element_type=F32)


def _dot_nt(a, b):
    return lax.dot_general(a, b, (((1,), (1,)), ((), ())), preferred_element_type=F32)


def _dot_tn(a, b):
    return lax.dot_general(a, b, (((0,), (0,)), ((), ())), preferred_element_type=F32)


def _block_diag(a):
    w = a.shape[1] // 2
    z = jnp.zeros_like(a[:, :w])
    return jnp.concatenate([jnp.concatenate([a[:, :w], z], axis=1),
                            jnp.concatenate([z, a[:, w:]], axis=1)], axis=0)


def _resident(shape):
    nd = len(shape)
    return pl.BlockSpec(shape, lambda *_: (0,) * nd, pipeline_mode=pl.Buffered(1))


def _layer_weight(shape, layer):
    nd = len(shape)
    return pl.BlockSpec((None,) + tuple(shape), lambda *_: (layer,) + (0,) * nd,
                        pipeline_mode=pl.Buffered(1))


def _row_block(total, want):
    blk = min(total, want)
    while total % blk:
        blk -= CHUNK
    return blk


def _run(pieces):
    for piece in pieces:
        piece()


def _interleave(slots, fillers):
    for k, slot in enumerate(slots):
        slot()
        if k < len(fillers):
            _run(fillers[k])


def _mlp_pieces(x_fn, g_ref, wu_ref, wd_ref, fg_ref, out_fn, *, final):
    st = {}

    def up(j):
        if j == 0:
            st["x"] = x_fn()
            st["h"] = _rms(st["x"], g_ref[...]).astype(BF16)
            st["acc"] = st["x"]
        u = _dot(st["h"], wu_ref[:, j * FF_BLOCK:(j + 1) * FF_BLOCK])
        st[j] = jnp.square(jnp.maximum(u, 0.0)).astype(BF16)

    def down(j):
        st["acc"] = st["acc"] + _dot(st.pop(j), wd_ref[j * FF_BLOCK:(j + 1) * FF_BLOCK, :])
        if j == D_FF // FF_BLOCK - 1:
            acc = st["acc"]
            out_fn(_rms(acc, fg_ref[...]) if final else acc)

    n = D_FF // FF_BLOCK
    return ([functools.partial(up, j) for j in range(n)],
            [functools.partial(down, j) for j in range(n)])


def _mlp_kernel(x_ref, g_ref, wu_ref, wd_ref, fg_ref, o_ref, *, final):
    def store(v):
        o_ref[...] = v

    ups, downs = _mlp_pieces(lambda: x_ref[...], g_ref, wu_ref, wd_ref, fg_ref, store, final=final)
    for up, down in zip(ups, downs):
        up()
        down()


def _mlp(x, g, w_up, w_down, final_g, final, layer):
    m = x.shape[0]
    tm = _row_block(m, 512)
    return pl.pallas_call(
        functools.partial(_mlp_kernel, final=final),
        grid=(m // tm,),
        in_specs=[
            pl.BlockSpec((tm, D_MODEL), lambda i: (i, 0)),
            _resident((1, D_MODEL)),
            _layer_weight((D_MODEL, D_FF), layer),
            _layer_weight((D_FF, D_MODEL), layer),
            _resident((1, D_MODEL)),
        ],
        out_specs=pl.BlockSpec((tm, D_MODEL), lambda i: (i, 0)),
        out_shape=jax.ShapeDtypeStruct((m, D_MODEL), F32),
        compiler_params=pltpu.CompilerParams(
            dimension_semantics=("arbitrary",), vmem_limit_bytes=VMEM_LIMIT_BYTES),
        name="mlp",
    )(x, g, w_up, w_down, final_g)


def _swa_project_pieces(x_fn, g_ref, wqkv_ref, tabs, ck_ref, cv_ref, bufs, *, nb, tq):
    q_scr, ka_scr, kb_scr, va_scr, vb_scr = bufs
    cos_ref, sa_ref, sb_ref = tabs
    st = {}

    def hidden():
        if "h" not in st:
            st["h"] = _rms(x_fn(), g_ref[...]).astype(BF16)
        return st["h"]

    def rope(xs):
        return (xs * cos_ref[...] + pltpu.roll(xs, ROT_DIM // 2, 1) * sa_ref[...]
                + pltpu.roll(xs, LANES - ROT_DIM // 2, 1) * sb_ref[...])

    scale = HEAD_DIM ** -0.5

    def q_piece(j):
        qb = _dot(hidden(), wqkv_ref[:, j * Q_BLOCK:(j + 1) * Q_BLOCK])
        for i in range(nb):
            rs = slice(i * tq, (i + 1) * tq)
            for s in range(Q_BLOCK // LANES):
                qs = rope(qb[rs, s * LANES:(s + 1) * LANES]) * scale
                c0 = j * Q_BLOCK + s * LANES
                q_scr[rs, c0:c0 + LANES] = qs.astype(BF16)

    def kv_piece():
        kv = _dot(hidden(), wqkv_ref[:, Q_DIM:Q_DIM + 2 * KV_DIM])
        low = lax.broadcasted_iota(jnp.int32, (1, LANES), 1) < HEAD_DIM
        for i in range(nb):
            rs = slice(i * tq, (i + 1) * tq)
            for s in range(KV_DIM // LANES):
                ls = slice(s * LANES, (s + 1) * LANES)
                knew = rope(kv[rs, ls])
                vnew = kv[rs, KV_DIM + s * LANES:KV_DIM + (s + 1) * LANES]
                kall = jnp.concatenate([ck_ref[i, :, ls], knew], axis=0)
                vall = jnp.concatenate([cv_ref[i, :, ls], vnew], axis=0)
                ck_ref[i, :, ls] = kall[tq:tq + WINDOW]
                cv_ref[i, :, ls] = vall[tq:tq + WINDOW]
                krot = pltpu.roll(kall, HEAD_DIM, 1)
                vrot = pltpu.roll(vall, HEAD_DIM, 1)
                zero = jnp.zeros_like(kall)
                ka_scr[i, 2 * s] = jnp.where(low, kall, zero).astype(BF16)
                kb_scr[i, 2 * s] = jnp.where(low, zero, krot).astype(BF16)
                ka_scr[i, 2 * s + 1] = jnp.where(low, krot, zero).astype(BF16)
                kb_scr[i, 2 * s + 1] = jnp.where(low, zero, kall).astype(BF16)
                va_scr[i, 2 * s] = jnp.where(low, vall, zero).astype(BF16)
                vb_scr[i, 2 * s] = jnp.where(low, zero, vrot).astype(BF16)
                va_scr[i, 2 * s + 1] = jnp.where(low, vrot, zero).astype(BF16)
                vb_scr[i, 2 * s + 1] = jnp.where(low, zero, vall).astype(BF16)

    return [functools.partial(q_piece, j) for j in range(Q_DIM // Q_BLOCK)] + [kv_piece]


def _swa_core_slots(bufs, o_scr, sink_ref, key_pos0, *, nb, tq):
    q_scr, ka_scr, kb_scr, va_scr, vb_scr = bufs
    nch = tq // CHUNK
    work = [(i, c) for i in range(nb) for c in range(nch)]
    key_iota = lax.broadcasted_iota(jnp.int32, (1, KEYS), 1)
    st = {}

    def softmax_sink(s, valid, sink):
        s = jnp.where(valid, s, -jnp.inf)
        m = jnp.maximum(jnp.max(s, axis=-1, keepdims=True), sink)
        e = jnp.exp(s - m)
        den = jnp.sum(e, axis=-1, keepdims=True) + jnp.exp(sink - m)
        return (e / den).astype(BF16)

    spk = GROUP // 2

    def scores(w):
        i, c = work[w]
        r0, k0 = i * tq + c * CHUNK, c * CHUNK
        out = []
        for hh in range(N_KV):
            qs = jnp.concatenate(
                [q_scr[r0:r0 + CHUNK, s * LANES:(s + 1) * LANES]
                 for s in range(hh * spk, (hh + 1) * spk)], axis=0)
            out.append((_dot_nt(qs, ka_scr[i, hh, k0:k0 + KEYS, :]),
                        _dot_nt(qs, kb_scr[i, hh, k0:k0 + KEYS, :])))
        st[w] = out

    def attend(w):
        i, c = work[w]
        r0, k0 = i * tq + c * CHUNK, c * CHUNK
        sc = st.pop(w)
        valid = (key_pos0 + k0 + key_iota) >= 0
        for hh in range(N_KV):
            pa, pb = [], []
            for j in range(spk):
                slab = hh * spk + j
                rs = slice(j * CHUNK, (j + 1) * CHUNK)
                pa.append(softmax_sink(sc[hh][0][rs], valid, sink_ref[2 * slab]))
                pb.append(softmax_sink(sc[hh][1][rs], valid, sink_ref[2 * slab + 1]))
            o = (_dot(jnp.concatenate(pa, axis=0), va_scr[i, hh, k0:k0 + KEYS, :])
                 + _dot(jnp.concatenate(pb, axis=0), vb_scr[i, hh, k0:k0 + KEYS, :]))
            for j in range(spk):
                slab = hh * spk + j
                o_scr[r0:r0 + CHUNK, slab * LANES:(slab + 1) * LANES] = (
                    o[j * CHUNK:(j + 1) * CHUNK].astype(BF16))

    def slot(w):
        if w < len(work):
            scores(w)
        if w > 0:
            attend(w - 1)

    return [functools.partial(slot, w) for w in range(len(work) + 1)]


def _attn_kernel(sink_ref, x_ref, g_ref, wqkv_ref, wo_ref, cos_ref, sa_ref, sb_ref,
                 ck0_ref, cv0_ref, y_ref, nk_ref, nv_ref,
                 q_scr, ka_scr, kb_scr, va_scr, vb_scr, o_scr, *, nb, tq, pos0):
    t = pl.program_id(1)

    @pl.when(t == 0)
    def _():
        nk_ref[...] = ck0_ref[...]
        nv_ref[...] = cv0_ref[...]

    bufs = (q_scr, ka_scr, kb_scr, va_scr, vb_scr)
    x = x_ref[...].reshape(nb * tq, D_MODEL)
    _run(_swa_project_pieces(lambda: x, g_ref, wqkv_ref, (cos_ref, sa_ref, sb_ref),
                             nk_ref, nv_ref, bufs, nb=nb, tq=tq))
    _run(_swa_core_slots(bufs, o_scr, sink_ref, pos0 + t * tq - WINDOW, nb=nb, tq=tq))
    y = x + _dot(o_scr[...], wo_ref[...])
    y_ref[...] = y.reshape(nb, tq, D_MODEL)


def _swa_scratch(nb, tq):
    kvlen = WINDOW + tq
    return [pltpu.VMEM((nb * tq, Q_DIM), BF16)] + [
        pltpu.VMEM((nb, N_KV, kvlen, LANES), BF16) for _ in range(4)]


def _attn_layer(x, g, w_qkv, w_o, sinks, rope_tabs, cache_k, cache_v, *, nb, tq, pos0, layer):
    b, t_len, _ = x.shape
    cos, sa, sb = rope_tabs
    cache_blk = pl.BlockSpec((nb, WINDOW, KV_DIM), lambda bi, ti: (bi, 0, 0))
    tab_blk = pl.BlockSpec((tq, LANES), lambda bi, ti: (ti, 0))
    return pl.pallas_call(
        functools.partial(_attn_kernel, nb=nb, tq=tq, pos0=pos0),
        grid=(b // nb, t_len // tq),
        in_specs=[
            pl.BlockSpec(memory_space=pltpu.SMEM),
            pl.BlockSpec((nb, tq, D_MODEL), lambda bi, ti: (bi, ti, 0)),
            _resident((1, D_MODEL)),
            _layer_weight((D_MODEL, Q_DIM + 2 * KV_DIM), layer),
            _layer_weight((Q_DIM, D_MODEL), layer),
            tab_blk, tab_blk, tab_blk,
            cache_blk, cache_blk,
        ],
        out_specs=[
            pl.BlockSpec((nb, tq, D_MODEL), lambda bi, ti: (bi, ti, 0)),
            cache_blk, cache_blk,
        ],
        out_shape=[
            jax.ShapeDtypeStruct((b, t_len, D_MODEL), F32),
            jax.ShapeDtypeStruct((b, WINDOW, KV_DIM), F32),
            jax.ShapeDtypeStruct((b, WINDOW, KV_DIM), F32),
        ],
        scratch_shapes=_swa_scratch(nb, tq) + [pltpu.VMEM((nb * tq, Q_DIM), BF16)],
        compiler_params=pltpu.CompilerParams(
            dimension_semantics=("arbitrary", "arbitrary"), vmem_limit_bytes=VMEM_LIMIT_BYTES),
        name="swa_layer",
    )(sinks, x, g, w_qkv, w_o, cos, sa, sb, cache_k, cache_v)


def _rope_tables(pos):
    half = ROT_DIM // 2
    inv_freq = ROPE_THETA ** (-(jnp.arange(half, dtype=F32) * 2.0) / ROT_DIM)
    ang = pos[:, None] * inv_freq[None, :]
    cos, sin = jnp.cos(ang), jnp.sin(ang)
    n = pos.shape[0]
    rest = HEAD_DIM - ROT_DIM
    c64 = jnp.concatenate([cos, cos, jnp.ones((n, rest), F32)], axis=1)
    sa64 = jnp.concatenate([jnp.zeros((n, half), F32), sin, jnp.zeros((n, rest), F32)], axis=1)
    sb64 = jnp.concatenate([-sin, jnp.zeros((n, half + rest), F32)], axis=1)
    rep = LANES // HEAD_DIM
    return tuple(jnp.tile(a, (1, rep)) for a in (c64, sa64, sb64))


def _segment_matrix():
    tril = np.tril(np.ones((CHUNK, CHUNK), np.float32))
    return np.concatenate([tril, tril, tril], axis=1)


def _hgrn_project_pieces(x_fn, g_ref, win_ref, z_scr):
    st = {}

    def piece(j):
        if "h" not in st:
            st["h"] = _rms(x_fn(), g_ref[...]).astype(BF16)
        cs_ = slice(j * PROJ_BLOCK, (j + 1) * PROJ_BLOCK)
        z_scr[:, cs_] = _dot(st["h"], win_ref[:, cs_])

    return [functools.partial(piece, j) for j in range(HG_IN // PROJ_BLOCK)]


def _hgrn_core_slots(z_scr, st_ref, o_scr, lbraw_ref, onorm_ref, seg_ref, *, nb, tq, layer):
    nch = tq // CHUNK
    work = [(i, c) for i in range(nb) for c in range(nch)]
    st = {}

    def consts():
        if "lb" in st:
            return
        lbraw = lbraw_ref[...]
        e = jnp.exp(lbraw - jnp.max(lbraw, axis=0, keepdims=True))
        sm = e / jnp.sum(e, axis=0, keepdims=True)
        cs0 = sm[0:1]
        cs = cs0
        for r in range(1, layer + 1):
            cs = cs + sm[r:r + 1]
        st["lb"] = cs - cs0
        st["one_m_lb"] = 1.0 - st["lb"]
        ti = lax.broadcasted_iota(jnp.int32, (CHUNK, 2 * CHUNK), 0)
        si = lax.broadcasted_iota(jnp.int32, (CHUNK, 2 * CHUNK), 1) & (CHUNK - 1)
        masks = [ti == si]
        for l in range(N_LEVELS):
            masks.append(((ti >> (l + 1)) == (si >> (l + 1)))
                         & (((ti >> l) & 1) == 1) & (((si >> l) & 1) == 0))
        st["masks"] = masks

    def intra(w):
        consts()
        i, c = work[w]
        lb, one_m_lb, masks = st["lb"], st["one_m_lb"], st["masks"]
        r0 = i * tq + c * CHUNK
        zq = z_scr[r0:r0 + CHUNK, 0:HG_F]
        zf = z_scr[r0:r0 + CHUNK, HG_F:2 * HG_F]
        zi = z_scr[r0:r0 + CHUNK, 2 * HG_F:2 * HG_F + D_MODEL]
        zg = z_scr[r0:r0 + CHUNK, 2 * HG_F + D_MODEL:]
        q = zq * (1.0 / (1.0 + jnp.exp(-zq)))
        et = jnp.exp(-jnp.abs(zf))
        rt = 1.0 / (1.0 + et)
        nonneg = zf >= 0.0
        sig_pos = jnp.where(nonneg, rt, et * rt)
        sig_neg = jnp.where(nonneg, et * rt, rt)
        fgate = lb + one_m_lb * sig_pos
        kk = one_m_lb * sig_neg
        gate = zg * (1.0 / (1.0 + jnp.exp(-zg)))
        l2f = jnp.log(fgate) * LOG2_E
        hi = l2f.astype(BF16)
        r1 = l2f - hi.astype(F32)
        mid = r1.astype(BF16)
        lo = (r1 - mid.astype(F32)).astype(BF16)
        b = _dot(seg_ref[...], jnp.concatenate([hi, mid, lo], axis=0))
        row = lax.broadcasted_iota(jnp.int32, (CHUNK, 1), 0)
        elev = [jnp.where((row & 1) == 1, fgate, 1.0)]
        for l in range(1, N_LEVELS):
            n = 1 << l
            if 2 * n >= 8:
                bref = jnp.concatenate(
                    [jnp.broadcast_to(b[blk + n - 1:blk + n, :], (2 * n, HG_F))
                     for blk in range(0, CHUNK, 2 * n)], axis=0)
            else:
                sub = lax.broadcasted_iota(jnp.int32, (8, 1), 0)
                bref = jnp.concatenate(
                    [jnp.where(((sub >> (l + 1)) & 1) == 0,
                               jnp.broadcast_to(b[g + n - 1:g + n, :], (8, HG_F)),
                               jnp.broadcast_to(b[g + 3 * n - 1:g + 3 * n, :], (8, HG_F)))
                     for g in range(0, CHUNK, 8)], axis=0)
            sgn = jnp.where(((row >> l) & 1) == 1, 1.0, -1.0)
            elev.append(jnp.exp2((b - bref) * sgn))
        eb_all = jnp.exp2(b)
        er_all = jnp.exp2(b[CHUNK - 1:CHUNK, :] - b)
        pairs = []
        for p in range(HG_HEADS // 2):
            cs_ = slice(2 * p * HG_EXPAND, (2 * p + 2) * HG_EXPAND)
            qp, kp = q[:, cs_], kk[:, cs_]
            sc = jnp.where(masks[0], _dot_nt(qp.astype(BF16), _block_diag(kp.astype(BF16))), 0.0)
            for l in range(N_LEVELS):
                el = elev[l][:, cs_]
                pr = _dot_nt((qp * el).astype(BF16), _block_diag((kp * el).astype(BF16)))
                sc = jnp.where(masks[l + 1], pr, sc)
            eb, er = eb_all[:, cs_], er_all[:, cs_]
            eb_last = jnp.broadcast_to(eb[CHUNK - 1:CHUNK, :], (8, 2 * HG_EXPAND))
            pairs.append(dict(
                sc=sc.astype(BF16), qe=(qp * eb).astype(BF16), ke=(kp * er).astype(BF16),
                v=zi[:, cs_].astype(BF16), gate=gate[:, cs_],
                eb_col=jnp.transpose(eb_last)[:, 0:1]))
        st[w] = pairs

    def inter(w):
        i, c = work[w]
        pairs = st.pop(w)
        onorm = onorm_ref[...]
        r0 = i * tq + c * CHUNK
        states = [st_ref[i, hh] for hh in range(HG_HEADS)]
        o_list = []
        for p, pd in enumerate(pairs):
            sbd = _block_diag(jnp.concatenate(
                [states[2 * p].astype(BF16), states[2 * p + 1].astype(BF16)], axis=1))
            o_list.append(_dot(pd["sc"], _block_diag(pd["v"])) + _dot(pd["qe"], sbd))
        new_states = []
        for hh in range(HG_HEADS):
            pd, cs_ = pairs[hh // 2], slice((hh % 2) * HG_EXPAND, (hh % 2 + 1) * HG_EXPAND)
            new_states.append(pd["eb_col"][cs_] * states[hh]
                              + _dot_tn(pd["ke"][:, cs_], pd["v"][:, cs_]))
        for hh in range(HG_HEADS):
            cs_ = slice((hh % 2) * HG_EXPAND, (hh % 2 + 1) * HG_EXPAND)
            o = o_list[hh // 2][:, cs_]
            on = o * lax.rsqrt(jnp.mean(o * o, axis=-1, keepdims=True) + EPS) * onorm
            st_ref[i, hh] = new_states[hh]
            o_scr[r0:r0 + CHUNK, hh * HG_EXPAND:(hh + 1) * HG_EXPAND] = (
                on * pairs[hh // 2]["gate"][:, cs_]).astype(BF16)

    def slot(w):
        if w < len(work):
            intra(w)
        if w > 0:
            inter(w - 1)

    return [functools.partial(slot, w) for w in range(len(work) + 1)]


def _hgrn_kernel(x_ref, g_ref, win_ref, lbraw_ref, onorm_ref, wo_ref, seg_ref, s0_ref,
                 y_ref, st_ref, z_scr, o_scr, *, nb, tq, layer):
    t = pl.program_id(1)

    @pl.when(t == 0)
    def _():
        st_ref[...] = s0_ref[...]

    x = x_ref[...].reshape(nb * tq, D_MODEL)
    _run(_hgrn_project_pieces(lambda: x, g_ref, win_ref, z_scr))
    _run(_hgrn_core_slots(z_scr, st_ref, o_scr, lbraw_ref, onorm_ref, seg_ref,
                          nb=nb, tq=tq, layer=layer))
    y = x + _dot(o_scr[...], wo_ref[...])
    y_ref[...] = y.reshape(nb, tq, D_MODEL)


def _hgrn_layer(x, g, w_in, lb_raw, out_norm, w_o, seg, s0, *, nb, tq, layer):
    b, t_len, _ = x.shape
    n_hgrn = lb_raw.shape[0]
    st_blk = pl.BlockSpec((nb, HG_HEADS, HG_EXPAND, HG_EXPAND), lambda bi, ti: (bi, 0, 0, 0))
    return pl.pallas_call(
        functools.partial(_hgrn_kernel, nb=nb, tq=tq, layer=layer),
        grid=(b // nb, t_len // tq),
        in_specs=[
            pl.BlockSpec((nb, tq, D_MODEL), lambda bi, ti: (bi, ti, 0)),
            _resident((1, D_MODEL)),
            _layer_weight((D_MODEL, HG_IN), layer),
            _resident((n_hgrn, HG_F)),
            _resident((1, HG_EXPAND)),
            _layer_weight((D_MODEL, D_MODEL), layer),
            _resident((CHUNK, 3 * CHUNK)),
            st_blk,
        ],
        out_specs=[
            pl.BlockSpec((nb, tq, D_MODEL), lambda bi, ti: (bi, ti, 0)),
            st_blk,
        ],
        out_shape=[
            jax.ShapeDtypeStruct((b, t_len, D_MODEL), F32),
            jax.ShapeDtypeStruct((b, HG_HEADS, HG_EXPAND, HG_EXPAND), F32),
        ],
        scratch_shapes=[
            pltpu.VMEM((nb * tq, HG_IN), F32),
            pltpu.VMEM((nb * tq, D_MODEL), BF16),
        ],
        compiler_params=pltpu.CompilerParams(
            dimension_semantics=("arbitrary", "arbitrary"), vmem_limit_bytes=VMEM_LIMIT_BYTES),
        name="hgrn_layer",
    )(x, g, w_in, lb_raw, out_norm, w_o, seg, s0)


def _pipeline_fillers(a_pieces, ups, downs, tail, n_slots):
    n_ff = len(ups)
    seq = []
    for j in range(max(len(a_pieces), n_ff) + 1):
        if j < len(a_pieces):
            seq.append(a_pieces[j])
        if j < n_ff:
            seq.append(ups[j])
        if 1 <= j <= n_ff:
            seq.append(downs[j - 1])
    if n_ff:
        tail = [seq.pop()] + tail
    gaps = n_slots - 1
    fillers = [seq[g * len(seq) // gaps:(g + 1) * len(seq) // gaps] for g in range(gaps)]
    return fillers + [tail]


def _fused_swa_kernel(sink_ref, xa_ref, xb_ref, g_ref, wqkv_ref, wo_ref, cos_ref, sa_ref, sb_ref,
                      gm_ref, wu_ref, wd_ref, fg_ref, out_ref, nk_ref, nv_ref,
                      *scr, tq, nt, nblk, final):
    set0, set1 = scr[0:5], scr[5:10]
    y1_0, y1_1, o_scr, ck_scr, cv_scr = scr[10:15]
    s = pl.program_id(0)
    t_a = jnp.minimum(s, nblk - 1) % nt
    t_b = jnp.clip(s - 1, 0, nblk - 1) % nt

    @pl.when(s == 0)
    def _():
        for ref in set1 + (y1_0,):
            ref[...] = jnp.zeros_like(ref)

    @pl.when(t_a == 0)
    def _():
        ck_scr[...] = jnp.zeros_like(ck_scr)
        cv_scr[...] = jnp.zeros_like(cv_scr)

    def step(wset, rset, y1w, y1r):
        a_pieces = _swa_project_pieces(lambda: xa_ref[...], g_ref, wqkv_ref,
                                       (cos_ref, sa_ref, sb_ref), ck_scr, cv_scr, wset,
                                       nb=1, tq=tq)
        slots = _swa_core_slots(rset, o_scr, sink_ref, t_b * tq - WINDOW, nb=1, tq=tq)

        def store(v):
            out_ref[...] = v

        ups, downs = _mlp_pieces(lambda: y1r[...], gm_ref, wu_ref, wd_ref, fg_ref, store,
                                 final=final)

        def out_proj():
            y1w[...] = xb_ref[...] + _dot(o_scr[...], wo_ref[...])

        _interleave(slots, _pipeline_fillers(a_pieces, ups, downs, [out_proj], len(slots)))

    @pl.when(s % 2 == 0)
    def _():
        step(set0, set1, y1_1, y1_0)

    @pl.when(s % 2 == 1)
    def _():
        step(set1, set0, y1_0, y1_1)

    @pl.when(t_a == nt - 1)
    def _():
        nk_ref[...] = ck_scr[...]
        nv_ref[...] = cv_scr[...]


def _piped_hgrn_kernel(xa_ref, xb_ref, g_ref, win_ref, lbraw_ref, onorm_ref, wo_ref, seg_ref,
                       out_ref, st_ref, z0, z1, o_scr, st_scr, *, tq, nt, nblk, layer):
    s = pl.program_id(0)
    t_b = jnp.clip(s - 1, 0, nblk - 1) % nt

    @pl.when(s == 0)
    def _():
        z1[...] = jnp.zeros_like(z1)

    @pl.when(t_b == 0)
    def _():
        st_scr[...] = jnp.zeros_like(st_scr)

    def step(zw, zr):
        a_pieces = _hgrn_project_pieces(lambda: xa_ref[...], g_ref, win_ref, zw)
        slots = _hgrn_core_slots(zr, st_scr, o_scr, lbraw_ref, onorm_ref, seg_ref,
                                 nb=1, tq=tq, layer=layer)

        def out_proj():
            out_ref[...] = xb_ref[...] + _dot(o_scr[...], wo_ref[...])

        _interleave(slots, _pipeline_fillers(a_pieces, [], [], [out_proj], len(slots)))

    @pl.when(s % 2 == 0)
    def _():
        step(z0, z1)

    @pl.when(s % 2 == 1)
    def _():
        step(z1, z0)

    @pl.when((t_b == nt - 1) & (s >= 1))
    def _():
        st_ref[...] = st_scr[...]


def _piped_hgrn_layer(x, tq, g, w_in, lb_raw, onorm, w_o, seg, *, layer):
    b, t_len, _ = x.shape
    nt = t_len // tq
    nblk = b * nt
    x2 = x.reshape(b * t_len, D_MODEL)

    def blk_b(s):
        return jnp.clip(s - 1, 0, nblk - 1)

    y, st = pl.pallas_call(
        functools.partial(_piped_hgrn_kernel, tq=tq, nt=nt, nblk=nblk, layer=layer),
        grid=(nblk + 1,),
        in_specs=[pl.BlockSpec((tq, D_MODEL), lambda s: (jnp.minimum(s, nblk - 1), 0)),
                  pl.BlockSpec((tq, D_MODEL), lambda s: (blk_b(s), 0)),
                  _resident((1, D_MODEL)),
                  _layer_weight((D_MODEL, HG_IN), layer),
                  _resident((lb_raw.shape[0], HG_F)), _resident((1, HG_EXPAND)),
                  _layer_weight((D_MODEL, D_MODEL), layer),
                  _resident((CHUNK, 3 * CHUNK))],
        out_specs=[pl.BlockSpec((tq, D_MODEL), lambda s: (blk_b(s), 0)),
                   pl.BlockSpec((1, HG_HEADS, HG_EXPAND, HG_EXPAND),
                                lambda s: (blk_b(s) // nt, 0, 0, 0))],
        out_shape=[jax.ShapeDtypeStruct((b * t_len, D_MODEL), F32),
                   jax.ShapeDtypeStruct((b, HG_HEADS, HG_EXPAND, HG_EXPAND), F32)],
        scratch_shapes=[pltpu.VMEM((tq, HG_IN), F32), pltpu.VMEM((tq, HG_IN), F32),
                        pltpu.VMEM((tq, D_MODEL), BF16),
                        pltpu.VMEM((1, HG_HEADS, HG_EXPAND, HG_EXPAND), F32)],
        compiler_params=pltpu.CompilerParams(dimension_semantics=("arbitrary",),
                                             vmem_limit_bytes=VMEM_LIMIT_BYTES),
        name="hgrn_piped_layer",
    )(x2, x2, g, w_in, lb_raw, onorm, w_o, seg)
    return y.reshape(b, t_len, D_MODEL), st


def _fused_swa_layer(x, tq, mixer_args, mlp_args, *, layer, mlp_layer, final):
    b, t_len, _ = x.shape
    nt = t_len // tq
    nblk = b * nt
    x2 = x.reshape(b * t_len, D_MODEL)

    def blk_a(s):
        return jnp.minimum(s, nblk - 1)

    g, w_qkv, w_o, sinks, (cos, sa, sb) = mixer_args
    row_blk = lambda f: pl.BlockSpec((tq, D_MODEL), lambda s: (f(s), 0))
    tab = pl.BlockSpec((tq, LANES), lambda s: (blk_a(s) % nt, 0))
    cache_blk = pl.BlockSpec((1, WINDOW, KV_DIM), lambda s: (blk_a(s) // nt, 0, 0))
    cache_shape = jax.ShapeDtypeStruct((b, WINDOW, KV_DIM), F32)
    y, nk, nv = pl.pallas_call(
        functools.partial(_fused_swa_kernel, tq=tq, nt=nt, nblk=nblk, final=final),
        grid=(nblk + 2,),
        in_specs=[pl.BlockSpec(memory_space=pltpu.SMEM), row_blk(blk_a),
                  row_blk(lambda s: jnp.clip(s - 1, 0, nblk - 1)),
                  _resident((1, D_MODEL)),
                  _layer_weight((D_MODEL, Q_DIM + 2 * KV_DIM), layer),
                  _layer_weight((Q_DIM, D_MODEL), layer), tab, tab, tab,
                  _resident((1, D_MODEL)), _layer_weight((D_MODEL, D_FF), mlp_layer),
                  _layer_weight((D_FF, D_MODEL), mlp_layer), _resident((1, D_MODEL))],
        out_specs=[row_blk(lambda s: jnp.clip(s - 2, 0, nblk - 1)), cache_blk, cache_blk],
        out_shape=[jax.ShapeDtypeStruct((b * t_len, D_MODEL), F32), cache_shape, cache_shape],
        scratch_shapes=(_swa_scratch(1, tq) + _swa_scratch(1, tq)
                        + [pltpu.VMEM((tq, D_MODEL), F32), pltpu.VMEM((tq, D_MODEL), F32),
                           pltpu.VMEM((tq, D_MODEL), BF16)]
                        + [pltpu.VMEM((1, WINDOW, KV_DIM), F32) for _ in range(2)]),
        compiler_params=pltpu.CompilerParams(dimension_semantics=("arbitrary",),
                                             vmem_limit_bytes=VMEM_LIMIT_BYTES),
        name="swa_mlp_layer",
    )(sinks, x2, x2, g, w_qkv, w_o, cos, sa, sb, *mlp_args)
    return y.reshape(b, t_len, D_MODEL), nk, nv


def kernel(x_prompt, x_sample, cache_k, cache_v, state_s, mixer_norm, mlp_norm, attn_w_qkv,
           attn_w_o, attn_sinks, hgrn_w_in, hgrn_lb, hgrn_out_norm, hgrn_w_o, mlp_w_up,
           mlp_w_down, final_norm):
    bp, tp, _ = x_prompt.shape
    bs, ts, _ = x_sample.shape
    assert cache_k.shape[2] == WINDOW and ts % CHUNK == 0 and tp % CHUNK == 0

    w_qkv = attn_w_qkv.astype(BF16)
    w_ao = attn_w_o.astype(BF16)
    w_in = hgrn_w_in.astype(BF16)
    w_ho = hgrn_w_o.astype(BF16)
    w_up = mlp_w_up.astype(BF16)
    w_down = mlp_w_down.astype(BF16)
    lb_raw = hgrn_lb.astype(F32)
    seg = jnp.asarray(_segment_matrix(), BF16)
    final_g = final_norm.reshape(1, D_MODEL)

    rope_p = _rope_tables(jnp.arange(tp, dtype=F32))
    rope_s = _rope_tables(PAST_LEN + jnp.arange(ts, dtype=F32))
    tq_p = _row_block(tp, 256)
    tq_h = _row_block(tp, 512)
    assert tq_p >= WINDOW

    yp, ys = x_prompt, x_sample
    kp_l, vp_l, sp_l, ks_l, vs_l, ss_l = [], [], [], [], [], []
    for i in range(DEPTH):
        j = i // 2
        g = mixer_norm[i].reshape(1, D_MODEL)
        gm = mlp_norm[i].reshape(1, D_MODEL)
        final = i == DEPTH - 1
        mlp_args = (gm, w_up, w_down, final_g)
        if i % 2 == 0:
            yp, kp, vp = _fused_swa_layer(yp, tq_p, (g, w_qkv, w_ao, attn_sinks[j], rope_p),
                                          mlp_args, layer=j, mlp_layer=i, final=final)
            ys, kn, vn = _attn_layer(ys, g, w_qkv, w_ao, attn_sinks[j], rope_s,
                                     cache_k[j].reshape(bs, WINDOW, KV_DIM),
                                     cache_v[j].reshape(bs, WINDOW, KV_DIM),
                                     nb=bs, tq=ts, pos0=PAST_LEN, layer=j)
            kp_l.append(kp.reshape(bp, WINDOW, N_KV, HEAD_DIM))
            vp_l.append(vp.reshape(bp, WINDOW, N_KV, HEAD_DIM))
            ks_l.append(kn.reshape(bs, WINDOW, N_KV, HEAD_DIM))
            vs_l.append(vn.reshape(bs, WINDOW, N_KV, HEAD_DIM))
        else:
            onorm = hgrn_out_norm[j].reshape(1, HG_EXPAND)
            yp, sp = _piped_hgrn_layer(yp, tq_h, g, w_in, lb_raw, onorm, w_ho, seg, layer=j)
            yp = _mlp(yp.reshape(bp * tp, D_MODEL), gm, w_up, w_down, final_g,
                      final, i).reshape(bp, tp, D_MODEL)
            ys, sn = _hgrn_layer(ys, g, w_in, lb_raw, onorm, w_ho, seg, state_s[j],
                                 nb=bs, tq=ts, layer=j)
            sp_l.append(sp)
            ss_l.append(sn)
        ys = _mlp(ys.reshape(bs * ts, D_MODEL), gm, w_up, w_down, final_g,
                  final, i).reshape(bs, ts, D_MODEL)
    return (yp, ys, jnp.stack(kp_l), jnp.stack(vp_l), jnp.stack(sp_l),
            jnp.stack(ks_l), jnp.stack(vs_l), jnp.stack(ss_l))
```

```python
import functools
import math

import jax
import jax.numpy as jnp
import numpy as np
from jax import lax
from jax.experimental import pallas as pl
from jax.experimental.pallas import tpu as pltpu

D_MODEL = 1024
DEPTH = 4
PAST_LEN = 2048
CHUNK = 64
N_HEADS = 16
N_KV = 4
HEAD_DIM = 64
GROUP = N_HEADS // N_KV
ROT_DIM = HEAD_DIM // 4
ROPE_THETA = 500000.0
WINDOW = 128
Q_DIM = N_HEADS * HEAD_DIM
KV_DIM = N_KV * HEAD_DIM
HG_EXPAND = 128
HG_HEADS = D_MODEL // HG_EXPAND
HG_F = HG_HEADS * HG_EXPAND
HG_IN = 2 * HG_F + 2 * D_MODEL
D_FF = 4 * D_MODEL
EPS = 1e-5

LANES = 128
VMEM_LIMIT_BYTES = 56 * 1024 * 1024

KEYS = WINDOW + CHUNK
N_LEVELS = 6
assert 1 << N_LEVELS == CHUNK
N_SLAB = Q_DIM // LANES
FF_BLOCK = 1024
PROJ_BLOCK = 1024
Q_BLOCK = 512
LOG2_E = math.log2(math.e)

F32 = jnp.float32
BF16 = jnp.bfloat16


def _rms(x, g):
    ms = jnp.mean(x * x, axis=-1, keepdims=True)
    return x * lax.rsqrt(ms + EPS) * g


def _dot(a, b):
    return jnp.dot(a, b, preferred_element_type=F32)


def _dot_nt(a, b):
    return lax.dot_general(a, b, (((1,), (1,)), ((), ())), preferred_element_type=F32)


def _dot_tn(a, b):
    return lax.dot_general(a, b, (((0,), (0,)), ((), ())), preferred_element_type=F32)


def _block_diag(a):
    w = a.shape[1] // 2
    z = jnp.zeros_like(a[:, :w])
    return jnp.concatenate([jnp.concatenate([a[:, :w], z], axis=1),
                            jnp.concatenate([z, a[:, w:]], axis=1)], axis=0)


def _resident(shape):
    nd = len(shape)
    return pl.BlockSpec(shape, lambda *_: (0,) * nd, pipeline_mode=pl.Buffered(1))


def _layer_weight(shape, layer):
    nd = len(shape)
    return pl.BlockSpec((None,) + tuple(shape), lambda *_: (layer,) + (0,) * nd,
                        pipeline_mode=pl.Buffered(1))


def _row_block(total, want):
    blk = min(total, want)
    while total % blk:
        blk -= CHUNK
    return blk


def _run(pieces):
    for piece in pieces:
        piece()


def _interleave(slots, fillers):
    for k, slot in enumerate(slots):
        slot()
        if k < len(fillers):
            _run(fillers[k])


def _mlp_pieces(x_fn, g_ref, wu_ref, wd_ref, fg_ref, out_fn, *, final):
    st = {}

    def up(j):
        if j == 0:
            st["x"] = x_fn()
            st["h"] = _rms(st["x"], g_ref[...]).astype(BF16)
            st["acc"] = st["x"]
        u = _dot(st["h"], wu_ref[:, j * FF_BLOCK:(j + 1) * FF_BLOCK])
        st[j] = jnp.square(jnp.maximum(u, 0.0)).astype(BF16)

    def down(j):
        st["acc"] = st["acc"] + _dot(st.pop(j), wd_ref[j * FF_BLOCK:(j + 1) * FF_BLOCK, :])
        if j == D_FF // FF_BLOCK - 1:
            acc = st["acc"]
            out_fn(_rms(acc, fg_ref[...]) if final else acc)

    n = D_FF // FF_BLOCK
    return ([functools.partial(up, j) for j in range(n)],
            [functools.partial(down, j) for j in range(n)])


def _mlp_kernel(x_ref, g_ref, wu_ref, wd_ref, fg_ref, o_ref, *, final):
    def store(v):
        o_ref[...] = v

    ups, downs = _mlp_pieces(lambda: x_ref[...], g_ref, wu_ref, wd_ref, fg_ref, store, final=final)
    for up, down in zip(ups, downs):
        up()
        down()


def _mlp(x, g, w_up, w_down, final_g, final, layer):
    m = x.shape[0]
    tm = _row_block(m, 512)
    return pl.pallas_call(
        functools.partial(_mlp_kernel, final=final),
        grid=(m // tm,),
        in_specs=[
            pl.BlockSpec((tm, D_MODEL), lambda i: (i, 0)),
            _resident((1, D_MODEL)),
            _layer_weight((D_MODEL, D_FF), layer),
            _layer_weight((D_FF, D_MODEL), layer),
            _resident((1, D_MODEL)),
        ],
        out_specs=pl.BlockSpec((tm, D_MODEL), lambda i: (i, 0)),
        out_shape=jax.ShapeDtypeStruct((m, D_MODEL), F32),
        compiler_params=pltpu.CompilerParams(
            dimension_semantics=("arbitrary",), vmem_limit_bytes=VMEM_LIMIT_BYTES),
        name="mlp",
    )(x, g, w_up, w_down, final_g)


def _swa_project_pieces(x_fn, g_ref, wqkv_ref, tabs, ck_ref, cv_ref, bufs, *, nb, tq):
    q_scr, ka_scr, kb_scr, va_scr, vb_scr = bufs
    cos_ref, sa_ref, sb_ref = tabs
    st = {}

    def hidden():
        if "h" not in st:
            st["h"] = _rms(x_fn(), g_ref[...]).astype(BF16)
        return st["h"]

    def rope(xs):
        return (xs * cos_ref[...] + pltpu.roll(xs, ROT_DIM // 2, 1) * sa_ref[...]
                + pltpu.roll(xs, LANES - ROT_DIM // 2, 1) * sb_ref[...])

    scale = HEAD_DIM ** -0.5

    def q_piece(j):
        qb = _dot(hidden(), wqkv_ref[:, j * Q_BLOCK:(j + 1) * Q_BLOCK])
        for i in range(nb):
            rs = slice(i * tq, (i + 1) * tq)
            for s in range(Q_BLOCK // LANES):
                qs = rope(qb[rs, s * LANES:(s + 1) * LANES]) * scale
                c0 = j * Q_BLOCK + s * LANES
                q_scr[rs, c0:c0 + LANES] = qs.astype(BF16)

    def kv_piece():
        kv = _dot(hidden(), wqkv_ref[:, Q_DIM:Q_DIM + 2 * KV_DIM])
        low = lax.broadcasted_iota(jnp.int32, (1, LANES), 1) < HEAD_DIM
        for i in range(nb):
            rs = slice(i * tq, (i + 1) * tq)
            for s in range(KV_DIM // LANES):
                ls = slice(s * LANES, (s + 1) * LANES)
                knew = rope(kv[rs, ls])
                vnew = kv[rs, KV_DIM + s * LANES:KV_DIM + (s + 1) * LANES]
                kall = jnp.concatenate([ck_ref[i, :, ls], knew], axis=0)
                vall = jnp.concatenate([cv_ref[i, :, ls], vnew], axis=0)
                ck_ref[i, :, ls] = kall[tq:tq + WINDOW]
                cv_ref[i, :, ls] = vall[tq:tq + WINDOW]
                krot = pltpu.roll(kall, HEAD_DIM, 1)
                vrot = pltpu.roll(vall, HEAD_DIM, 1)
                zero = jnp.zeros_like(kall)
                ka_scr[i, 2 * s] = jnp.where(low, kall, zero).astype(BF16)
                kb_scr[i, 2 * s] = jnp.where(low, zero, krot).astype(BF16)
                ka_scr[i, 2 * s + 1] = jnp.where(low, krot, zero).astype(BF16)
                kb_scr[i, 2 * s + 1] = jnp.where(low, zero, kall).astype(BF16)
                va_scr[i, 2 * s] = jnp.where(low, vall, zero).astype(BF16)
                vb_scr[i, 2 * s] = jnp.where(low, zero, vrot).astype(BF16)
                va_scr[i, 2 * s + 1] = jnp.where(low, vrot, zero).astype(BF16)
                vb_scr[i, 2 * s + 1] = jnp.where(low, zero, vall).astype(BF16)

    return [functools.partial(q_piece, j) for j in range(Q_DIM // Q_BLOCK)] + [kv_piece]


def _swa_core_slots(bufs, o_scr, sink_ref, key_pos0, *, nb, tq):
    q_scr, ka_scr, kb_scr, va_scr, vb_scr = bufs
    nch = tq // CHUNK
    work = [(i, c) for i in range(nb) for c in range(nch)]
    key_iota = lax.broadcasted_iota(jnp.int32, (1, KEYS), 1)
    st = {}

    def softmax_sink(s, valid, sink):
        s = jnp.where(valid, s, -jnp.inf)
        m = jnp.maximum(jnp.max(s, axis=-1, keepdims=True), sink)
        e = jnp.exp(s - m)
        den = jnp.sum(e, axis=-1, keepdims=True) + jnp.exp(sink - m)
        return (e / den).astype(BF16)

    spk = GROUP // 2

    def scores(w):
        i, c = work[w]
        r0, k0 = i * tq + c * CHUNK, c * CHUNK
        out = []
        for hh in range(N_KV):
            qs = jnp.concatenate(
                [q_scr[r0:r0 + CHUNK, s * LANES:(s + 1) * LANES]
                 for s in range(hh * spk, (hh + 1) * spk)], axis=0)
            out.append((_dot_nt(qs, ka_scr[i, hh, k0:k0 + KEYS, :]),
                        _dot_nt(qs, kb_scr[i, hh, k0:k0 + KEYS, :])))
        st[w] = out

    def attend(w):
        i, c = work[w]
        r0, k0 = i * tq + c * CHUNK, c * CHUNK
        sc = st.pop(w)
        valid = (key_pos0 + k0 + key_iota) >= 0
        for hh in range(N_KV):
            pa, pb = [], []
            for j in range(spk):
                slab = hh * spk + j
                rs = slice(j * CHUNK, (j + 1) * CHUNK)
                pa.append(softmax_sink(sc[hh][0][rs], valid, sink_ref[2 * slab]))
                pb.append(softmax_sink(sc[hh][1][rs], valid, sink_ref[2 * slab + 1]))
            o = (_dot(jnp.concatenate(pa, axis=0), va_scr[i, hh, k0:k0 + KEYS, :])
                 + _dot(jnp.concatenate(pb, axis=0), vb_scr[i, hh, k0:k0 + KEYS, :]))
            for j in range(spk):
                slab = hh * spk + j
                o_scr[r0:r0 + CHUNK, slab * LANES:(slab + 1) * LANES] = (
                    o[j * CHUNK:(j + 1) * CHUNK].astype(BF16))

    def slot(w):
        if w < len(work):
            scores(w)
        if w > 0:
            attend(w - 1)

    return [functools.partial(slot, w) for w in range(len(work) + 1)]


def _attn_kernel(sink_ref, x_ref, g_ref, wqkv_ref, wo_ref, cos_ref, sa_ref, sb_ref,
                 ck0_ref, cv0_ref, y_ref, nk_ref, nv_ref,
                 q_scr, ka_scr, kb_scr, va_scr, vb_scr, o_scr, *, nb, tq, pos0):
    t = pl.program_id(1)

    @pl.when(t == 0)
    def _():
        nk_ref[...] = ck0_ref[...]
        nv_ref[...] = cv0_ref[...]

    bufs = (q_scr, ka_scr, kb_scr, va_scr, vb_scr)
    x = x_ref[...].reshape(nb * tq, D_MODEL)
    _run(_swa_project_pieces(lambda: x, g_ref, wqkv_ref, (cos_ref, sa_ref, sb_ref),
                             nk_ref, nv_ref, bufs, nb=nb, tq=tq))
    _run(_swa_core_slots(bufs, o_scr, sink_ref, pos0 + t * tq - WINDOW, nb=nb, tq=tq))
    y = x + _dot(o_scr[...], wo_ref[...])
    y_ref[...] = y.reshape(nb, tq, D_MODEL)


def _swa_scratch(nb, tq):
    kvlen = WINDOW + tq
    return [pltpu.VMEM((nb * tq, Q_DIM), BF16)] + [
        pltpu.VMEM((nb, N_KV, kvlen, LANES), BF16) for _ in range(4)]


def _attn_layer(x, g, w_qkv, w_o, sinks, rope_tabs, cache_k, cache_v, *, nb, tq, pos0, layer):
    b, t_len, _ = x.shape
    cos, sa, sb = rope_tabs
    cache_blk = pl.BlockSpec((nb, WINDOW, KV_DIM), lambda bi, ti: (bi, 0, 0))
    tab_blk = pl.BlockSpec((tq, LANES), lambda bi, ti: (ti, 0))
    return pl.pallas_call(
        functools.partial(_attn_kernel, nb=nb, tq=tq, pos0=pos0),
        grid=(b // nb, t_len // tq),
        in_specs=[
            pl.BlockSpec(memory_space=pltpu.SMEM),
            pl.BlockSpec((nb, tq, D_MODEL), lambda bi, ti: (bi, ti, 0)),
            _resident((1, D_MODEL)),
            _layer_weight((D_MODEL, Q_DIM + 2 * KV_DIM), layer),
            _layer_weight((Q_DIM, D_MODEL), layer),
            tab_blk, tab_blk, tab_blk,
            cache_blk, cache_blk,
        ],
        out_specs=[
            pl.BlockSpec((nb, tq, D_MODEL), lambda bi, ti: (bi, ti, 0)),
            cache_blk, cache_blk,
        ],
        out_shape=[
            jax.ShapeDtypeStruct((b, t_len, D_MODEL), F32),
            jax.ShapeDtypeStruct((b, WINDOW, KV_DIM), F32),
            jax.ShapeDtypeStruct((b, WINDOW, KV_DIM), F32),
        ],
        scratch_shapes=_swa_scratch(nb, tq) + [pltpu.VMEM((nb * tq, Q_DIM), BF16)],
        compiler_params=pltpu.CompilerParams(
            dimension_semantics=("arbitrary", "arbitrary"), vmem_limit_bytes=VMEM_LIMIT_BYTES),
        name="swa_layer",
    )(sinks, x, g, w_qkv, w_o, cos, sa, sb, cache_k, cache_v)


def _rope_tables(pos):
    half = ROT_DIM // 2
    inv_freq = ROPE_THETA ** (-(jnp.arange(half, dtype=F32) * 2.0) / ROT_DIM)
    ang = pos[:, None] * inv_freq[None, :]
    cos, sin = jnp.cos(ang), jnp.sin(ang)
    n = pos.shape[0]
    rest = HEAD_DIM - ROT_DIM
    c64 = jnp.concatenate([cos, cos, jnp.ones((n, rest), F32)], axis=1)
    sa64 = jnp.concatenate([jnp.zeros((n, half), F32), sin, jnp.zeros((n, rest), F32)], axis=1)
    sb64 = jnp.concatenate([-sin, jnp.zeros((n, half + rest), F32)], axis=1)
    rep = LANES // HEAD_DIM
    return tuple(jnp.tile(a, (1, rep)) for a in (c64, sa64, sb64))


def _segment_matrix():
    tril = np.tril(np.ones((CHUNK, CHUNK), np.float32))
    return np.concatenate([tril, tril, tril], axis=1)


def _hgrn_project_pieces(x_fn, g_ref, win_ref, z_scr):
    st = {}

    def piece(j):
        if "h" not in st:
            st["h"] = _rms(x_fn(), g_ref[...]).astype(BF16)
        cs_ = slice(j * PROJ_BLOCK, (j + 1) * PROJ_BLOCK)
        z_scr[:, cs_] = _dot(st["h"], win_ref[:, cs_])

    return [functools.partial(piece, j) for j in range(HG_IN // PROJ_BLOCK)]


def _hgrn_core_slots(z_scr, st_ref, o_scr, lbraw_ref, onorm_ref, seg_ref, *, nb, tq, layer):
    nch = tq // CHUNK
    work = [(i, c) for i in range(nb) for c in range(nch)]
    st = {}

    def consts():
        if "lb" in st:
            return
        lbraw = lbraw_ref[...]
        e = jnp.exp(lbraw - jnp.max(lbraw, axis=0, keepdims=True))
        sm = e / jnp.sum(e, axis=0, keepdims=True)
        cs0 = sm[0:1]
        cs = cs0
        for r in range(1, layer + 1):
            cs = cs + sm[r:r + 1]
        st["lb"] = cs - cs0
        st["one_m_lb"] = 1.0 - st["lb"]
        ti = lax.broadcasted_iota(jnp.int32, (CHUNK, 2 * CHUNK), 0)
        si = lax.broadcasted_iota(jnp.int32, (CHUNK, 2 * CHUNK), 1) & (CHUNK - 1)
        masks = [ti == si]
        for l in range(N_LEVELS):
            masks.append(((ti >> (l + 1)) == (si >> (l + 1)))
                         & (((ti >> l) & 1) == 1) & (((si >> l) & 1) == 0))
        st["masks"] = masks

    def intra(w):
        consts()
        i, c = work[w]
        lb, one_m_lb, masks = st["lb"], st["one_m_lb"], st["masks"]
        r0 = i * tq + c * CHUNK
        zq = z_scr[r0:r0 + CHUNK, 0:HG_F]
        zf = z_scr[r0:r0 + CHUNK, HG_F:2 * HG_F]
        zi = z_scr[r0:r0 + CHUNK, 2 * HG_F:2 * HG_F + D_MODEL]
        zg = z_scr[r0:r0 + CHUNK, 2 * HG_F + D_MODEL:]
        q = zq * (1.0 / (1.0 + jnp.exp(-zq)))
        et = jnp.exp(-jnp.abs(zf))
        rt = 1.0 / (1.0 + et)
        nonneg = zf >= 0.0
        sig_pos = jnp.where(nonneg, rt, et * rt)
        sig_neg = jnp.where(nonneg, et * rt, rt)
        fgate = lb + one_m_lb * sig_pos
        kk = one_m_lb * sig_neg
        gate = zg * (1.0 / (1.0 + jnp.exp(-zg)))
        l2f = jnp.log(fgate) * LOG2_E
        hi = l2f.astype(BF16)
        r1 = l2f - hi.astype(F32)
        mid = r1.astype(BF16)
        lo = (r1 - mid.astype(F32)).astype(BF16)
        b = _dot(seg_ref[...], jnp.concatenate([hi, mid, lo], axis=0))
        row = lax.broadcasted_iota(jnp.int32, (CHUNK, 1), 0)
        elev = [jnp.where((row & 1) == 1, fgate, 1.0)]
        for l in range(1, N_LEVELS):
            n = 1 << l
            if 2 * n >= 8:
                bref = jnp.concatenate(
                    [jnp.broadcast_to(b[blk + n - 1:blk + n, :], (2 * n, HG_F))
                     for blk in range(0, CHUNK, 2 * n)], axis=0)
            else:
                sub = lax.broadcasted_iota(jnp.int32, (8, 1), 0)
                bref = jnp.concatenate(
                    [jnp.where(((sub >> (l + 1)) & 1) == 0,
                               jnp.broadcast_to(b[g + n - 1:g + n, :], (8, HG_F)),
                               jnp.broadcast_to(b[g + 3 * n - 1:g + 3 * n, :], (8, HG_F)))
                     for g in range(0, CHUNK, 8)], axis=0)
            sgn = jnp.where(((row >> l) & 1) == 1, 1.0, -1.0)
            elev.append(jnp.exp2((b - bref) * sgn))
        eb_all = jnp.exp2(b)
        er_all = jnp.exp2(b[CHUNK - 1:CHUNK, :] - b)
        pairs = []
        for p in range(HG_HEADS // 2):
            cs_ = slice(2 * p * HG_EXPAND, (2 * p + 2) * HG_EXPAND)
            qp, kp = q[:, cs_], kk[:, cs_]
            sc = jnp.where(masks[0], _dot_nt(qp.astype(BF16), _block_diag(kp.astype(BF16))), 0.0)
            for l in range(N_LEVELS):
                el = elev[l][:, cs_]
                pr = _dot_nt((qp * el).astype(BF16), _block_diag((kp * el).astype(BF16)))
                sc = jnp.where(masks[l + 1], pr, sc)
            eb, er = eb_all[:, cs_], er_all[:, cs_]
            eb_last = jnp.broadcast_to(eb[CHUNK - 1:CHUNK, :], (8, 2 * HG_EXPAND))
            pairs.append(dict(
                sc=sc.astype(BF16), qe=(qp * eb).astype(BF16), ke=(kp * er).astype(BF16),
                v=zi[:, cs_].astype(BF16), gate=gate[:, cs_],
                eb_col=jnp.transpose(eb_last)[:, 0:1]))
        st[w] = pairs

    def inter(w):
        i, c = work[w]
        pairs = st.pop(w)
        onorm = onorm_ref[...]
        r0 = i * tq + c * CHUNK
        states = [st_ref[i, hh] for hh in range(HG_HEADS)]
        o_list = []
        for p, pd in enumerate(pairs):
            sbd = _block_diag(jnp.concatenate(
                [states[2 * p].astype(BF16), states[2 * p + 1].astype(BF16)], axis=1))
            o_list.append(_dot(pd["sc"], _block_diag(pd["v"])) + _dot(pd["qe"], sbd))
        new_states = []
        for hh in range(HG_HEADS):
            pd, cs_ = pairs[hh // 2], slice((hh % 2) * HG_EXPAND, (hh % 2 + 1) * HG_EXPAND)
            new_states.append(pd["eb_col"][cs_] * states[hh]
                              + _dot_tn(pd["ke"][:, cs_], pd["v"][:, cs_]))
        for hh in range(HG_HEADS):
            cs_ = slice((hh % 2) * HG_EXPAND, (hh % 2 + 1) * HG_EXPAND)
            o = o_list[hh // 2][:, cs_]
            on = o * lax.rsqrt(jnp.mean(o * o, axis=-1, keepdims=True) + EPS) * onorm
            st_ref[i, hh] = new_states[hh]
            o_scr[r0:r0 + CHUNK, hh * HG_EXPAND:(hh + 1) * HG_EXPAND] = (
                on * pairs[hh // 2]["gate"][:, cs_]).astype(BF16)

    def slot(w):
        if w < len(work):
            intra(w)
        if w > 0:
            inter(w - 1)

    return [functools.partial(slot, w) for w in range(len(work) + 1)]


def _hgrn_kernel(x_ref, g_ref, win_ref, lbraw_ref, onorm_ref, wo_ref, seg_ref, s0_ref,
                 y_ref, st_ref, z_scr, o_scr, *, nb, tq, layer):
    t = pl.program_id(1)

    @pl.when(t == 0)
    def _():
        st_ref[...] = s0_ref[...]

    x = x_ref[...].reshape(nb * tq, D_MODEL)
    _run(_hgrn_project_pieces(lambda: x, g_ref, win_ref, z_scr))
    _run(_hgrn_core_slots(z_scr, st_ref, o_scr, lbraw_ref, onorm_ref, seg_ref,
                          nb=nb, tq=tq, layer=layer))
    y = x + _dot(o_scr[...], wo_ref[...])
    y_ref[...] = y.reshape(nb, tq, D_MODEL)


def _hgrn_layer(x, g, w_in, lb_raw, out_norm, w_o, seg, s0, *, nb, tq, layer):
    b, t_len, _ = x.shape
    n_hgrn = lb_raw.shape[0]
    st_blk = pl.BlockSpec((nb, HG_HEADS, HG_EXPAND, HG_EXPAND), lambda bi, ti: (bi, 0, 0, 0))
    return pl.pallas_call(
        functools.partial(_hgrn_kernel, nb=nb, tq=tq, layer=layer),
        grid=(b // nb, t_len // tq),
        in_specs=[
            pl.BlockSpec((nb, tq, D_MODEL), lambda bi, ti: (bi, ti, 0)),
            _resident((1, D_MODEL)),
            _layer_weight((D_MODEL, HG_IN), layer),
            _resident((n_hgrn, HG_F)),
            _resident((1, HG_EXPAND)),
            _layer_weight((D_MODEL, D_MODEL), layer),
            _resident((CHUNK, 3 * CHUNK)),
            st_blk,
        ],
        out_specs=[
            pl.BlockSpec((nb, tq, D_MODEL), lambda bi, ti: (bi, ti, 0)),
            st_blk,
        ],
        out_shape=[
            jax.ShapeDtypeStruct((b, t_len, D_MODEL), F32),
            jax.ShapeDtypeStruct((b, HG_HEADS, HG_EXPAND, HG_EXPAND), F32),
        ],
        scratch_shapes=[
            pltpu.VMEM((nb * tq, HG_IN), F32),
            pltpu.VMEM((nb * tq, D_MODEL), BF16),
        ],
        compiler_params=pltpu.CompilerParams(
            dimension_semantics=("arbitrary", "arbitrary"), vmem_limit_bytes=VMEM_LIMIT_BYTES),
        name="hgrn_layer",
    )(x, g, w_in, lb_raw, out_norm, w_o, seg, s0)


def _pipeline_fillers(a_pieces, ups, downs, tail, n_slots):
    n_ff = len(ups)
    seq = []
    for j in range(max(len(a_pieces), n_ff) + 1):
        if j < len(a_pieces):
            seq.append(a_pieces[j])
        if j < n_ff:
            seq.append(ups[j])
        if 1 <= j <= n_ff:
            seq.append(downs[j - 1])
    if n_ff:
        tail = [seq.pop()] + tail
    gaps = n_slots - 1
    fillers = [seq[g * len(seq) // gaps:(g + 1) * len(seq) // gaps] for g in range(gaps)]
    return fillers + [tail]


def _fused_swa_kernel(sink_ref, xa_ref, xb_ref, g_ref, wqkv_ref, wo_ref, cos_ref, sa_ref, sb_ref,
                      gm_ref, wu_ref, wd_ref, fg_ref, out_ref, nk_ref, nv_ref,
                      *scr, tq, nt, nblk, final):
    set0, set1 = scr[0:5], scr[5:10]
    y1_0, y1_1, o_scr, ck_scr, cv_scr = scr[10:15]
    s = pl.program_id(0)
    t_a = jnp.minimum(s, nblk - 1) % nt
    t_b = jnp.clip(s - 1, 0, nblk - 1) % nt

    @pl.when(s == 0)
    def _():
        for ref in set1 + (y1_0,):
            ref[...] = jnp.zeros_like(ref)

    @pl.when(t_a == 0)
    def _():
        ck_scr[...] = jnp.zeros_like(ck_scr)
        cv_scr[...] = jnp.zeros_like(cv_scr)

    def step(wset, rset, y1w, y1r):
        a_pieces = _swa_project_pieces(lambda: xa_ref[...], g_ref, wqkv_ref,
                                       (cos_ref, sa_ref, sb_ref), ck_scr, cv_scr, wset,
                                       nb=1, tq=tq)
        slots = _swa_core_slots(rset, o_scr, sink_ref, t_b * tq - WINDOW, nb=1, tq=tq)

        def store(v):
            out_ref[...] = v

        ups, downs = _mlp_pieces(lambda: y1r[...], gm_ref, wu_ref, wd_ref, fg_ref, store,
                                 final=final)

        def out_proj():
            y1w[...] = xb_ref[...] + _dot(o_scr[...], wo_ref[...])

        _interleave(slots, _pipeline_fillers(a_pieces, ups, downs, [out_proj], len(slots)))

    @pl.when(s % 2 == 0)
    def _():
        step(set0, set1, y1_1, y1_0)

    @pl.when(s % 2 == 1)
    def _():
        step(set1, set0, y1_0, y1_1)

    @pl.when(t_a == nt - 1)
    def _():
        nk_ref[...] = ck_scr[...]
        nv_ref[...] = cv_scr[...]


def _piped_hgrn_kernel(xa_ref, xb_ref, g_ref, win_ref, lbraw_ref, onorm_ref, wo_ref, seg_ref,
                       out_ref, st_ref, z0, z1, o_scr, st_scr, *, tq, nt, nblk, layer):
    s = pl.program_id(0)
    t_b = jnp.clip(s - 1, 0, nblk - 1) % nt

    @pl.when(s == 0)
    def _():
        z1[...] = jnp.zeros_like(z1)

    @pl.when(t_b == 0)
    def _():
        st_scr[...] = jnp.zeros_like(st_scr)

    def step(zw, zr):
        a_pieces = _hgrn_project_pieces(lambda: xa_ref[...], g_ref, win_ref, zw)
        slots = _hgrn_core_slots(zr, st_scr, o_scr, lbraw_ref, onorm_ref, seg_ref,
                                 nb=1, tq=tq, layer=layer)

        def out_proj():
            out_ref[...] = xb_ref[...] + _dot(o_scr[...], wo_ref[...])

        _interleave(slots, _pipeline_fillers(a_pieces, [], [], [out_proj], len(slots)))

    @pl.when(s % 2 == 0)
    def _():
        step(z0, z1)

    @pl.when(s % 2 == 1)
    def _():
        step(z1, z0)

    @pl.when((t_b == nt - 1) & (s >= 1))
    def _():
        st_ref[...] = st_scr[...]


def _piped_hgrn_layer(x, tq, g, w_in, lb_raw, onorm, w_o, seg, *, layer):
    b, t_len, _ = x.shape
    nt = t_len // tq
    nblk = b * nt
    x2 = x.reshape(b * t_len, D_MODEL)

    def blk_b(s):
        return jnp.clip(s - 1, 0, nblk - 1)

    y, st = pl.pallas_call(
        functools.partial(_piped_hgrn_kernel, tq=tq, nt=nt, nblk=nblk, layer=layer),
        grid=(nblk + 1,),
        in_specs=[pl.BlockSpec((tq, D_MODEL), lambda s: (jnp.minimum(s, nblk - 1), 0)),
                  pl.BlockSpec((tq, D_MODEL), lambda s: (blk_b(s), 0)),
                  _resident((1, D_MODEL)),
                  _layer_weight((D_MODEL, HG_IN), layer),
                  _resident((lb_raw.shape[0], HG_F)), _resident((1, HG_EXPAND)),
                  _layer_weight((D_MODEL, D_MODEL), layer),
                  _resident((CHUNK, 3 * CHUNK))],
        out_specs=[pl.BlockSpec((tq, D_MODEL), lambda s: (blk_b(s), 0)),
                   pl.BlockSpec((1, HG_HEADS, HG_EXPAND, HG_EXPAND),
                                lambda s: (blk_b(s) // nt, 0, 0, 0))],
        out_shape=[jax.ShapeDtypeStruct((b * t_len, D_MODEL), F32),
                   jax.ShapeDtypeStruct((b, HG_HEADS, HG_EXPAND, HG_EXPAND), F32)],
        scratch_shapes=[pltpu.VMEM((tq, HG_IN), F32), pltpu.VMEM((tq, HG_IN), F32),
                        pltpu.VMEM((tq, D_MODEL), BF16),
                        pltpu.VMEM((1, HG_HEADS, HG_EXPAND, HG_EXPAND), F32)],
        compiler_params=pltpu.CompilerParams(dimension_semantics=("arbitrary",),
                                             vmem_limit_bytes=VMEM_LIMIT_BYTES),
        name="hgrn_piped_layer",
    )(x2, x2, g, w_in, lb_raw, onorm, w_o, seg)
    return y.reshape(b, t_len, D_MODEL), st


def _fused_swa_layer(x, tq, mixer_args, mlp_args, *, layer, mlp_layer, final):
    b, t_len, _ = x.shape
    nt = t_len // tq
    nblk = b * nt
    x2 = x.reshape(b * t_len, D_MODEL)

    def blk_a(s):
        return jnp.minimum(s, nblk - 1)

    g, w_qkv, w_o, sinks, (cos, sa, sb) = mixer_args
    row_blk = lambda f: pl.BlockSpec((tq, D_MODEL), lambda s: (f(s), 0))
    tab = pl.BlockSpec((tq, LANES), lambda s: (blk_a(s) % nt, 0))
    cache_blk = pl.BlockSpec((1, WINDOW, KV_DIM), lambda s: (blk_a(s) // nt, 0, 0))
    cache_shape = jax.ShapeDtypeStruct((b, WINDOW, KV_DIM), F32)
    y, nk, nv = pl.pallas_call(
        functools.partial(_fused_swa_kernel, tq=tq, nt=nt, nblk=nblk, final=final),
        grid=(nblk + 2,),
        in_specs=[pl.BlockSpec(memory_space=pltpu.SMEM), row_blk(blk_a),
                  row_blk(lambda s: jnp.clip(s - 1, 0, nblk - 1)),
                  _resident((1, D_MODEL)),
                  _layer_weight((D_MODEL, Q_DIM + 2 * KV_DIM), layer),
                  _layer_weight((Q_DIM, D_MODEL), layer), tab, tab, tab,
                  _resident((1, D_MODEL)), _layer_weight((D_MODEL, D_FF), mlp_layer),
                  _layer_weight((D_FF, D_MODEL), mlp_layer), _resident((1, D_MODEL))],
        out_specs=[row_blk(lambda s: jnp.clip(s - 2, 0, nblk - 1)), cache_blk, cache_blk],
        out_shape=[jax.ShapeDtypeStruct((b * t_len, D_MODEL), F32), cache_shape, cache_shape],
        scratch_shapes=(_swa_scratch(1, tq) + _swa_scratch(1, tq)
                        + [pltpu.VMEM((tq, D_MODEL), F32), pltpu.VMEM((tq, D_MODEL), F32),
                           pltpu.VMEM((tq, D_MODEL), BF16)]
                        + [pltpu.VMEM((1, WINDOW, KV_DIM), F32) for _ in range(2)]),
        compiler_params=pltpu.CompilerParams(dimension_semantics=("arbitrary",),
                                             vmem_limit_bytes=VMEM_LIMIT_BYTES),
        name="swa_mlp_layer",
    )(sinks, x2, x2, g, w_qkv, w_o, cos, sa, sb, *mlp_args)
    return y.reshape(b, t_len, D_MODEL), nk, nv


def kernel(x_prompt, x_sample, cache_k, cache_v, state_s, mixer_norm, mlp_norm, attn_w_qkv,
           attn_w_o, attn_sinks, hgrn_w_in, hgrn_lb, hgrn_out_norm, hgrn_w_o, mlp_w_up,
           mlp_w_down, final_norm):
    bp, tp, _ = x_prompt.shape
    bs, ts, _ = x_sample.shape
    assert cache_k.shape[2] == WINDOW and ts % CHUNK == 0 and tp % CHUNK == 0

    w_qkv = attn_w_qkv.astype(BF16)
    w_ao = attn_w_o.astype(BF16)
    w_in = hgrn_w_in.astype(BF16)
    w_ho = hgrn_w_o.astype(BF16)
    w_up = mlp_w_up.astype(BF16)
    w_down = mlp_w_down.astype(BF16)
    lb_raw = hgrn_lb.astype(F32)
    seg = jnp.asarray(_segment_matrix(), BF16)
    final_g = final_norm.reshape(1, D_MODEL)

    rope_p = _rope_tables(jnp.arange(tp, dtype=F32))
    rope_s = _rope_tables(PAST_LEN + jnp.arange(ts, dtype=F32))
    tq_p = _row_block(tp, 256)
    tq_h = _row_block(tp, 256)
    assert tq_p >= WINDOW

    yp, ys = x_prompt, x_sample
    kp_l, vp_l, sp_l, ks_l, vs_l, ss_l = [], [], [], [], [], []
    for i in range(DEPTH):
        j = i // 2
        g = mixer_norm[i].reshape(1, D_MODEL)
        gm = mlp_norm[i].reshape(1, D_MODEL)
        final = i == DEPTH - 1
        mlp_args = (gm, w_up, w_down, final_g)
        if i % 2 == 0:
            yp, kp, vp = _fused_swa_layer(yp, tq_p, (g, w_qkv, w_ao, attn_sinks[j], rope_p),
                                          mlp_args, layer=j, mlp_layer=i, final=final)
            ys, kn, vn = _attn_layer(ys, g, w_qkv, w_ao, attn_sinks[j], rope_s,
                                     cache_k[j].reshape(bs, WINDOW, KV_DIM),
                                     cache_v[j].reshape(bs, WINDOW, KV_DIM),
                                     nb=bs, tq=ts, pos0=PAST_LEN, layer=j)
            kp_l.append(kp.reshape(bp, WINDOW, N_KV, HEAD_DIM))
            vp_l.append(vp.reshape(bp, WINDOW, N_KV, HEAD_DIM))
            ks_l.append(kn.reshape(bs, WINDOW, N_KV, HEAD_DIM))
            vs_l.append(vn.reshape(bs, WINDOW, N_KV, HEAD_DIM))
        else:
            onorm = hgrn_out_norm[j].reshape(1, HG_EXPAND)
            yp, sp = _piped_hgrn_layer(yp, tq_h, g, w_in, lb_raw, onorm, w_ho, seg, layer=j)
            yp = _mlp(yp.reshape(bp * tp, D_MODEL), gm, w_up, w_down, final_g,
                      final, i).reshape(bp, tp, D_MODEL)
            ys, sn = _hgrn_layer(ys, g, w_in, lb_raw, onorm, w_ho, seg, state_s[j],
                                 nb=bs, tq=ts, layer=j)
            sp_l.append(sp)
            ss_l.append(sn)
        ys = _mlp(ys.reshape(bs * ts, D_MODEL), gm, w_up, w_down, final_g,
                  final, i).reshape(bs, ts, D_MODEL)
    return (yp, ys, jnp.stack(kp_l), jnp.stack(vp_l), jnp.stack(sp_l),
            jnp.stack(ks_l), jnp.stack(vs_l), jnp.stack(ss_l))
```

```python
import functools
import math

import jax
import jax.numpy as jnp
import numpy as np
from jax import lax
from jax.experimental import pallas as pl
from jax.experimental.pallas import tpu as pltpu

D_MODEL = 1024
DEPTH = 4
PAST_LEN = 2048
CHUNK = 64
N_HEADS = 16
N_KV = 4
HEAD_DIM = 64
GROUP = N_HEADS // N_KV
ROT_DIM = HEAD_DIM // 4
ROPE_THETA = 500000.0
WINDOW = 128
Q_DIM = N_HEADS * HEAD_DIM
KV_DIM = N_KV * HEAD_DIM
HG_EXPAND = 128
HG_HEADS = D_MODEL // HG_EXPAND
HG_F = HG_HEADS * HG_EXPAND
HG_IN = 2 * HG_F + 2 * D_MODEL
D_FF = 4 * D_MODEL
EPS = 1e-5

LANES = 128
VMEM_LIMIT_BYTES = 56 * 1024 * 1024

KEYS = WINDOW + CHUNK
N_LEVELS = 6
assert 1 << N_LEVELS == CHUNK
N_SLAB = Q_DIM // LANES
FF_BLOCK = 1024
PROJ_BLOCK = 1024
Q_BLOCK = 512
LOG2_E = math.log2(math.e)

F32 = jnp.float32
BF16 = jnp.bfloat16


def _rms(x, g):
    ms = jnp.mean(x * x, axis=-1, keepdims=True)
    return x * lax.rsqrt(ms + EPS) * g


def _dot(a, b):
    return jnp.dot(a, b, preferred_element_type=F32)


def _dot_nt(a, b):
    return lax.dot_general(a, b, (((1,), (1,)), ((), ())), preferred_element_type=F32)


def _dot_tn(a, b):
    return lax.dot_general(a, b, (((0,), (0,)), ((), ())), preferred_element_type=F32)


def _block_diag(a):
    w = a.shape[1] // 2
    z = jnp.zeros_like(a[:, :w])
    return jnp.concatenate([jnp.concatenate([a[:, :w], z], axis=1),
                            jnp.concatenate([z, a[:, w:]], axis=1)], axis=0)


def _resident(shape):
    nd = len(shape)
    return pl.BlockSpec(shape, lambda *_: (0,) * nd, pipeline_mode=pl.Buffered(1))


def _layer_weight(shape, layer):
    nd = len(shape)
    return pl.BlockSpec((None,) + tuple(shape), lambda *_: (layer,) + (0,) * nd,
                        pipeline_mode=pl.Buffered(1))


def _row_block(total, want):
    blk = min(total, want)
    while total % blk:
        blk -= CHUNK
    return blk


def _run(pieces):
    for piece in pieces:
        piece()


def _interleave(slots, fillers):
    for k, slot in enumerate(slots):
        slot()
        if k < len(fillers):
            _run(fillers[k])


def _mlp_pieces(x_fn, g_ref, wu_ref, wd_ref, fg_ref, out_fn, *, final):
    st = {}

    def up(j):
        if j == 0:
            st["x"] = x_fn()
            st["h"] = _rms(st["x"], g_ref[...]).astype(BF16)
            st["acc"] = st["x"]
        u = _dot(st["h"], wu_ref[:, j * FF_BLOCK:(j + 1) * FF_BLOCK])
        st[j] = jnp.square(jnp.maximum(u, 0.0)).astype(BF16)

    def down(j):
        st["acc"] = st["acc"] + _dot(st.pop(j), wd_ref[j * FF_BLOCK:(j + 1) * FF_BLOCK, :])
        if j == D_FF // FF_BLOCK - 1:
            acc = st["acc"]
            out_fn(_rms(acc, fg_ref[...]) if final else acc)

    n = D_FF // FF_BLOCK
    return ([functools.partial(up, j) for j in range(n)],
            [functools.partial(down, j) for j in range(n)])


def _mlp_kernel(x_ref, g_ref, wu_ref, wd_ref, fg_ref, o_ref, *, final):
    def store(v):
        o_ref[...] = v

    ups, downs = _mlp_pieces(lambda: x_ref[...], g_ref, wu_ref, wd_ref, fg_ref, store, final=final)
    for up, down in zip(ups, downs):
        up()
        down()


def _mlp(x, g, w_up, w_down, final_g, final, layer):
    m = x.shape[0]
    tm = _row_block(m, 512)
    return pl.pallas_call(
        functools.partial(_mlp_kernel, final=final),
        grid=(m // tm,),
        in_specs=[
            pl.BlockSpec((tm, D_MODEL), lambda i: (i, 0)),
            _resident((1, D_MODEL)),
            _layer_weight((D_MODEL, D_FF), layer),
            _layer_weight((D_FF, D_MODEL), layer),
            _resident((1, D_MODEL)),
        ],
        out_specs=pl.BlockSpec((tm, D_MODEL), lambda i: (i, 0)),
        out_shape=jax.ShapeDtypeStruct((m, D_MODEL), F32),
        compiler_params=pltpu.CompilerParams(
            dimension_semantics=("arbitrary",), vmem_limit_bytes=VMEM_LIMIT_BYTES),
        name="mlp",
    )(x, g, w_up, w_down, final_g)


def _swa_project_pieces(x_fn, g_ref, wqkv_ref, tabs, ck_ref, cv_ref, bufs, *, nb, tq):
    q_scr, kk_scr, vv_scr = bufs
    cos_ref, sa_ref, sb_ref = tabs
    st = {}

    def hidden():
        if "h" not in st:
            st["h"] = _rms(x_fn(), g_ref[...]).astype(BF16)
        return st["h"]

    def rope(xs):
        return (xs * cos_ref[...] + pltpu.roll(xs, ROT_DIM // 2, 1) * sa_ref[...]
                + pltpu.roll(xs, LANES - ROT_DIM // 2, 1) * sb_ref[...])

    scale = HEAD_DIM ** -0.5

    def q_piece(j):
        qb = _dot(hidden(), wqkv_ref[:, j * Q_BLOCK:(j + 1) * Q_BLOCK])
        for i in range(nb):
            rs = slice(i * tq, (i + 1) * tq)
            for s in range(Q_BLOCK // LANES):
                qs = rope(qb[rs, s * LANES:(s + 1) * LANES]) * scale
                c0 = j * Q_BLOCK + s * LANES
                q_scr[rs, c0:c0 + LANES] = qs.astype(BF16)

    def kv_piece():
        kv = _dot(hidden(), wqkv_ref[:, Q_DIM:Q_DIM + 2 * KV_DIM])
        low = lax.broadcasted_iota(jnp.int32, (1, LANES), 1) < HEAD_DIM
        for i in range(nb):
            rs = slice(i * tq, (i + 1) * tq)
            for s in range(KV_DIM // LANES):
                ls = slice(s * LANES, (s + 1) * LANES)
                knew = rope(kv[rs, ls])
                vnew = kv[rs, KV_DIM + s * LANES:KV_DIM + (s + 1) * LANES]
                kall = jnp.concatenate([ck_ref[i, :, ls], knew], axis=0)
                vall = jnp.concatenate([cv_ref[i, :, ls], vnew], axis=0)
                ck_ref[i, :, ls] = kall[tq:tq + WINDOW]
                cv_ref[i, :, ls] = vall[tq:tq + WINDOW]
                krot = pltpu.roll(kall, HEAD_DIM, 1)
                vrot = pltpu.roll(vall, HEAD_DIM, 1)
                kk_scr[i, 2 * s] = jnp.where(low, kall, krot).astype(BF16)
                kk_scr[i, 2 * s + 1] = jnp.where(low, krot, kall).astype(BF16)
                vv_scr[i, 2 * s] = jnp.where(low, vall, vrot).astype(BF16)
                vv_scr[i, 2 * s + 1] = jnp.where(low, vrot, vall).astype(BF16)

    return [functools.partial(q_piece, j) for j in range(Q_DIM // Q_BLOCK)] + [kv_piece]


def _swa_core_slots(bufs, o_scr, sink_ref, key_pos0, *, nb, tq):
    q_scr, kk_scr, vv_scr = bufs
    nch = tq // CHUNK
    work = [(i, c) for i in range(nb) for c in range(nch)]
    key_iota = lax.broadcasted_iota(jnp.int32, (1, KEYS), 1)
    st = {}

    def softmax_sink(s, valid, sink):
        s = jnp.where(valid, s, -jnp.inf)
        m = jnp.maximum(jnp.max(s, axis=-1, keepdims=True), sink)
        e = jnp.exp(s - m)
        den = jnp.sum(e, axis=-1, keepdims=True) + jnp.exp(sink - m)
        return (e / den).astype(BF16)

    spk = GROUP // 2
    low = lax.broadcasted_iota(jnp.int32, (1, LANES), 1) < HEAD_DIM

    def scores(w):
        i, c = work[w]
        r0, k0 = i * tq + c * CHUNK, c * CHUNK
        out = []
        for hh in range(N_KV):
            qs = jnp.concatenate(
                [q_scr[r0:r0 + CHUNK, s * LANES:(s + 1) * LANES]
                 for s in range(hh * spk, (hh + 1) * spk)], axis=0)
            zero = jnp.zeros_like(qs)
            q4 = jnp.concatenate([jnp.where(low, qs, zero), jnp.where(low, zero, qs)], axis=0)
            out.append(_dot_nt(q4, kk_scr[i, hh, k0:k0 + KEYS, :]))
        st[w] = out

    def attend(w):
        i, c = work[w]
        r0, k0 = i * tq + c * CHUNK, c * CHUNK
        sc = st.pop(w)
        valid = (key_pos0 + k0 + key_iota) >= 0
        for hh in range(N_KV):
            probs = []
            for half in range(2):
                for j in range(spk):
                    rs = slice((half * spk + j) * CHUNK, (half * spk + j + 1) * CHUNK)
                    probs.append(softmax_sink(sc[hh][rs], valid,
                                              sink_ref[2 * (hh * spk + j) + half]))
            o = _dot(jnp.concatenate(probs, axis=0), vv_scr[i, hh, k0:k0 + KEYS, :])
            for j in range(spk):
                slab = hh * spk + j
                o_lo = o[j * CHUNK:(j + 1) * CHUNK]
                o_hi = o[(spk + j) * CHUNK:(spk + j + 1) * CHUNK]
                o_scr[r0:r0 + CHUNK, slab * LANES:(slab + 1) * LANES] = (
                    jnp.where(low, o_lo, o_hi).astype(BF16))

    def slot(w):
        if w < len(work):
            scores(w)
        if w > 0:
            attend(w - 1)

    return [functools.partial(slot, w) for w in range(len(work) + 1)]


def _attn_kernel(sink_ref, x_ref, g_ref, wqkv_ref, wo_ref, cos_ref, sa_ref, sb_ref,
                 ck0_ref, cv0_ref, y_ref, nk_ref, nv_ref,
                 q_scr, kk_scr, vv_scr, o_scr, *, nb, tq, pos0):
    t = pl.program_id(1)

    @pl.when(t == 0)
    def _():
        nk_ref[...] = ck0_ref[...]
        nv_ref[...] = cv0_ref[...]

    bufs = (q_scr, kk_scr, vv_scr)
    x = x_ref[...].reshape(nb * tq, D_MODEL)
    _run(_swa_project_pieces(lambda: x, g_ref, wqkv_ref, (cos_ref, sa_ref, sb_ref),
                             nk_ref, nv_ref, bufs, nb=nb, tq=tq))
    _run(_swa_core_slots(bufs, o_scr, sink_ref, pos0 + t * tq - WINDOW, nb=nb, tq=tq))
    y = x + _dot(o_scr[...], wo_ref[...])
    y_ref[...] = y.reshape(nb, tq, D_MODEL)


def _swa_scratch(nb, tq):
    kvlen = WINDOW + tq
    return [pltpu.VMEM((nb * tq, Q_DIM), BF16)] + [
        pltpu.VMEM((nb, N_KV, kvlen, LANES), BF16) for _ in range(2)]


def _attn_layer(x, g, w_qkv, w_o, sinks, rope_tabs, cache_k, cache_v, *, nb, tq, pos0, layer):
    b, t_len, _ = x.shape
    cos, sa, sb = rope_tabs
    cache_blk = pl.BlockSpec((nb, WINDOW, KV_DIM), lambda bi, ti: (bi, 0, 0))
    tab_blk = pl.BlockSpec((tq, LANES), lambda bi, ti: (ti, 0))
    return pl.pallas_call(
        functools.partial(_attn_kernel, nb=nb, tq=tq, pos0=pos0),
        grid=(b // nb, t_len // tq),
        in_specs=[
            pl.BlockSpec(memory_space=pltpu.SMEM),
            pl.BlockSpec((nb, tq, D_MODEL), lambda bi, ti: (bi, ti, 0)),
            _resident((1, D_MODEL)),
            _layer_weight((D_MODEL, Q_DIM + 2 * KV_DIM), layer),
            _layer_weight((Q_DIM, D_MODEL), layer),
            tab_blk, tab_blk, tab_blk,
            cache_blk, cache_blk,
        ],
        out_specs=[
            pl.BlockSpec((nb, tq, D_MODEL), lambda bi, ti: (bi, ti, 0)),
            cache_blk, cache_blk,
        ],
        out_shape=[
            jax.ShapeDtypeStruct((b, t_len, D_MODEL), F32),
            jax.ShapeDtypeStruct((b, WINDOW, KV_DIM), F32),
            jax.ShapeDtypeStruct((b, WINDOW, KV_DIM), F32),
        ],
        scratch_shapes=_swa_scratch(nb, tq) + [pltpu.VMEM((nb * tq, Q_DIM), BF16)],
        compiler_params=pltpu.CompilerParams(
            dimension_semantics=("arbitrary", "arbitrary"), vmem_limit_bytes=VMEM_LIMIT_BYTES),
        name="swa_layer",
    )(sinks, x, g, w_qkv, w_o, cos, sa, sb, cache_k, cache_v)


def _rope_tables(pos):
    half = ROT_DIM // 2
    inv_freq = ROPE_THETA ** (-(jnp.arange(half, dtype=F32) * 2.0) / ROT_DIM)
    ang = pos[:, None] * inv_freq[None, :]
    cos, sin = jnp.cos(ang), jnp.sin(ang)
    n = pos.shape[0]
    rest = HEAD_DIM - ROT_DIM
    c64 = jnp.concatenate([cos, cos, jnp.ones((n, rest), F32)], axis=1)
    sa64 = jnp.concatenate([jnp.zeros((n, half), F32), sin, jnp.zeros((n, rest), F32)], axis=1)
    sb64 = jnp.concatenate([-sin, jnp.zeros((n, half + rest), F32)], axis=1)
    rep = LANES // HEAD_DIM
    return tuple(jnp.tile(a, (1, rep)) for a in (c64, sa64, sb64))


def _segment_matrix():
    tril = np.tril(np.ones((CHUNK, CHUNK), np.float32))
    return np.concatenate([tril, tril, tril], axis=1)


def _hgrn_project_pieces(x_fn, g_ref, win_ref, z_scr):
    st = {}

    def piece(j):
        if "h" not in st:
            st["h"] = _rms(x_fn(), g_ref[...]).astype(BF16)
        cs_ = slice(j * PROJ_BLOCK, (j + 1) * PROJ_BLOCK)
        z_scr[:, cs_] = _dot(st["h"], win_ref[:, cs_])

    return [functools.partial(piece, j) for j in range(HG_IN // PROJ_BLOCK)]


def _hgrn_core_slots(z_scr, st_ref, o_scr, lbraw_ref, onorm_ref, seg_ref, *, nb, tq, layer):
    nch = tq // CHUNK
    work = [(i, c) for i in range(nb) for c in range(nch)]
    st = {}

    def consts():
        if "lb" in st:
            return
        lbraw = lbraw_ref[...]
        e = jnp.exp(lbraw - jnp.max(lbraw, axis=0, keepdims=True))
        sm = e / jnp.sum(e, axis=0, keepdims=True)
        cs0 = sm[0:1]
        cs = cs0
        for r in range(1, layer + 1):
            cs = cs + sm[r:r + 1]
        st["lb"] = cs - cs0
        st["one_m_lb"] = 1.0 - st["lb"]
        ti = lax.broadcasted_iota(jnp.int32, (CHUNK, 2 * CHUNK), 0)
        si = lax.broadcasted_iota(jnp.int32, (CHUNK, 2 * CHUNK), 1) & (CHUNK - 1)
        masks = [ti == si]
        for l in range(N_LEVELS):
            masks.append(((ti >> (l + 1)) == (si >> (l + 1)))
                         & (((ti >> l) & 1) == 1) & (((si >> l) & 1) == 0))
        st["masks"] = masks

    def intra(w):
        consts()
        i, c = work[w]
        lb, one_m_lb, masks = st["lb"], st["one_m_lb"], st["masks"]
        r0 = i * tq + c * CHUNK
        zq = z_scr[r0:r0 + CHUNK, 0:HG_F]
        zf = z_scr[r0:r0 + CHUNK, HG_F:2 * HG_F]
        zi = z_scr[r0:r0 + CHUNK, 2 * HG_F:2 * HG_F + D_MODEL]
        zg = z_scr[r0:r0 + CHUNK, 2 * HG_F + D_MODEL:]
        q = zq * (1.0 / (1.0 + jnp.exp(-zq)))
        et = jnp.exp(-jnp.abs(zf))
        rt = 1.0 / (1.0 + et)
        nonneg = zf >= 0.0
        sig_pos = jnp.where(nonneg, rt, et * rt)
        sig_neg = jnp.where(nonneg, et * rt, rt)
        fgate = lb + one_m_lb * sig_pos
        kk = one_m_lb * sig_neg
        gate = zg * (1.0 / (1.0 + jnp.exp(-zg)))
        l2f = jnp.log(fgate) * LOG2_E
        hi = l2f.astype(BF16)
        r1 = l2f - hi.astype(F32)
        mid = r1.astype(BF16)
        lo = (r1 - mid.astype(F32)).astype(BF16)
        b = _dot(seg_ref[...], jnp.concatenate([hi, mid, lo], axis=0))
        row = lax.broadcasted_iota(jnp.int32, (CHUNK, 1), 0)
        elev = [jnp.where((row & 1) == 1, fgate, 1.0)]
        for l in range(1, N_LEVELS):
            n = 1 << l
            if 2 * n >= 8:
                bref = jnp.concatenate(
                    [jnp.broadcast_to(b[blk + n - 1:blk + n, :], (2 * n, HG_F))
                     for blk in range(0, CHUNK, 2 * n)], axis=0)
            else:
                sub = lax.broadcasted_iota(jnp.int32, (8, 1), 0)
                bref = jnp.concatenate(
                    [jnp.where(((sub >> (l + 1)) & 1) == 0,
                               jnp.broadcast_to(b[g + n - 1:g + n, :], (8, HG_F)),
                               jnp.broadcast_to(b[g + 3 * n - 1:g + 3 * n, :], (8, HG_F)))
                     for g in range(0, CHUNK, 8)], axis=0)
            sgn = jnp.where(((row >> l) & 1) == 1, 1.0, -1.0)
            elev.append(jnp.exp2((b - bref) * sgn))
        eb_all = jnp.exp2(b)
        er_all = jnp.exp2(b[CHUNK - 1:CHUNK, :] - b)
        pairs = []
        for p in range(HG_HEADS // 2):
            cs_ = slice(2 * p * HG_EXPAND, (2 * p + 2) * HG_EXPAND)
            qp, kp = q[:, cs_], kk[:, cs_]
            sc = jnp.where(masks[0], _dot_nt(qp.astype(BF16), _block_diag(kp.astype(BF16))), 0.0)
            for l in range(N_LEVELS):
                el = elev[l][:, cs_]
                pr = _dot_nt((qp * el).astype(BF16), _block_diag((kp * el).astype(BF16)))
                sc = jnp.where(masks[l + 1], pr, sc)
            eb, er = eb_all[:, cs_], er_all[:, cs_]
            eb_last = jnp.broadcast_to(eb[CHUNK - 1:CHUNK, :], (8, 2 * HG_EXPAND))
            pairs.append(dict(
                sc=sc.astype(BF16), qe=(qp * eb).astype(BF16), ke=(kp * er).astype(BF16),
                v=zi[:, cs_].astype(BF16), gate=gate[:, cs_],
                eb_col=jnp.transpose(eb_last)[:, 0:1]))
        st[w] = pairs

    def inter(w):
        i, c = work[w]
        pairs = st.pop(w)
        onorm = onorm_ref[...]
        r0 = i * tq + c * CHUNK
        states = [st_ref[i, hh] for hh in range(HG_HEADS)]
        o_list = []
        for p, pd in enumerate(pairs):
            sbd = _block_diag(jnp.concatenate(
                [states[2 * p].astype(BF16), states[2 * p + 1].astype(BF16)], axis=1))
            o_list.append(_dot(pd["sc"], _block_diag(pd["v"])) + _dot(pd["qe"], sbd))
        new_states = []
        for hh in range(HG_HEADS):
            pd, cs_ = pairs[hh // 2], slice((hh % 2) * HG_EXPAND, (hh % 2 + 1) * HG_EXPAND)
            new_states.append(pd["eb_col"][cs_] * states[hh]
                              + _dot_tn(pd["ke"][:, cs_], pd["v"][:, cs_]))
        for hh in range(HG_HEADS):
            cs_ = slice((hh % 2) * HG_EXPAND, (hh % 2 + 1) * HG_EXPAND)
            o = o_list[hh // 2][:, cs_]
            on = o * lax.rsqrt(jnp.mean(o * o, axis=-1, keepdims=True) + EPS) * onorm
            st_ref[i, hh] = new_states[hh]
            o_scr[r0:r0 + CHUNK, hh * HG_EXPAND:(hh + 1) * HG_EXPAND] = (
                on * pairs[hh // 2]["gate"][:, cs_]).astype(BF16)

    def slot(w):
        if w < len(work):
            intra(w)
        if w > 0:
            inter(w - 1)

    return [functools.partial(slot, w) for w in range(len(work) + 1)]


def _hgrn_kernel(x_ref, g_ref, win_ref, lbraw_ref, onorm_ref, wo_ref, seg_ref, s0_ref,
                 y_ref, st_ref, z_scr, o_scr, *, nb, tq, layer):
    t = pl.program_id(1)

    @pl.when(t == 0)
    def _():
        st_ref[...] = s0_ref[...]

    x = x_ref[...].reshape(nb * tq, D_MODEL)
    _run(_hgrn_project_pieces(lambda: x, g_ref, win_ref, z_scr))
    _run(_hgrn_core_slots(z_scr, st_ref, o_scr, lbraw_ref, onorm_ref, seg_ref,
                          nb=nb, tq=tq, layer=layer))
    y = x + _dot(o_scr[...], wo_ref[...])
    y_ref[...] = y.reshape(nb, tq, D_MODEL)


def _hgrn_layer(x, g, w_in, lb_raw, out_norm, w_o, seg, s0, *, nb, tq, layer):
    b, t_len, _ = x.shape
    n_hgrn = lb_raw.shape[0]
    st_blk = pl.BlockSpec((nb, HG_HEADS, HG_EXPAND, HG_EXPAND), lambda bi, ti: (bi, 0, 0, 0))
    return pl.pallas_call(
        functools.partial(_hgrn_kernel, nb=nb, tq=tq, layer=layer),
        grid=(b // nb, t_len // tq),
        in_specs=[
            pl.BlockSpec((nb, tq, D_MODEL), lambda bi, ti: (bi, ti, 0)),
            _resident((1, D_MODEL)),
            _layer_weight((D_MODEL, HG_IN), layer),
            _resident((n_hgrn, HG_F)),
            _resident((1, HG_EXPAND)),
            _layer_weight((D_MODEL, D_MODEL), layer),
            _resident((CHUNK, 3 * CHUNK)),
            st_blk,
        ],
        out_specs=[
            pl.BlockSpec((nb, tq, D_MODEL), lambda bi, ti: (bi, ti, 0)),
            st_blk,
        ],
        out_shape=[
            jax.ShapeDtypeStruct((b, t_len, D_MODEL), F32),
            jax.ShapeDtypeStruct((b, HG_HEADS, HG_EXPAND, HG_EXPAND), F32),
        ],
        scratch_shapes=[
            pltpu.VMEM((nb * tq, HG_IN), F32),
            pltpu.VMEM((nb * tq, D_MODEL), BF16),
        ],
        compiler_params=pltpu.CompilerParams(
            dimension_semantics=("arbitrary", "arbitrary"), vmem_limit_bytes=VMEM_LIMIT_BYTES),
        name="hgrn_layer",
    )(x, g, w_in, lb_raw, out_norm, w_o, seg, s0)


def _pipeline_fillers(a_pieces, ups, downs, tail, n_slots):
    n_ff = len(ups)
    seq = []
    for j in range(max(len(a_pieces), n_ff) + 1):
        if j < len(a_pieces):
            seq.append(a_pieces[j])
        if j < n_ff:
            seq.append(ups[j])
        if 1 <= j <= n_ff:
            seq.append(downs[j - 1])
    if n_ff:
        tail = [seq.pop()] + tail
    gaps = n_slots - 1
    fillers = [seq[g * len(seq) // gaps:(g + 1) * len(seq) // gaps] for g in range(gaps)]
    return fillers + [tail]


def _fused_swa_kernel(sink_ref, xa_ref, xb_ref, g_ref, wqkv_ref, wo_ref, cos_ref, sa_ref, sb_ref,
                      gm_ref, wu_ref, wd_ref, fg_ref, out_ref, nk_ref, nv_ref,
                      *scr, tq, nt, nblk, final):
    set0, set1 = scr[0:3], scr[3:6]
    y1_0, y1_1, o_scr, ck_scr, cv_scr = scr[6:11]
    s = pl.program_id(0)
    t_a = jnp.minimum(s, nblk - 1) % nt
    t_b = jnp.clip(s - 1, 0, nblk - 1) % nt

    @pl.when(s == 0)
    def _():
        for ref in set1 + (y1_0,):
            ref[...] = jnp.zeros_like(ref)

    @pl.when(t_a == 0)
    def _():
        ck_scr[...] = jnp.zeros_like(ck_scr)
        cv_scr[...] = jnp.zeros_like(cv_scr)

    def step(wset, rset, y1w, y1r):
        a_pieces = _swa_project_pieces(lambda: xa_ref[...], g_ref, wqkv_ref,
                                       (cos_ref, sa_ref, sb_ref), ck_scr, cv_scr, wset,
                                       nb=1, tq=tq)
        slots = _swa_core_slots(rset, o_scr, sink_ref, t_b * tq - WINDOW, nb=1, tq=tq)

        def store(v):
            out_ref[...] = v

        ups, downs = _mlp_pieces(lambda: y1r[...], gm_ref, wu_ref, wd_ref, fg_ref, store,
                                 final=final)

        def out_proj():
            y1w[...] = xb_ref[...] + _dot(o_scr[...], wo_ref[...])

        _interleave(slots, _pipeline_fillers(a_pieces, ups, downs, [out_proj], len(slots)))

    @pl.when(s % 2 == 0)
    def _():
        step(set0, set1, y1_1, y1_0)

    @pl.when(s % 2 == 1)
    def _():
        step(set1, set0, y1_0, y1_1)

    @pl.when(t_a == nt - 1)
    def _():
        nk_ref[...] = ck_scr[...]
        nv_ref[...] = cv_scr[...]


def _piped_hgrn_kernel(xa_ref, xb_ref, g_ref, win_ref, lbraw_ref, onorm_ref, wo_ref, seg_ref,
                       out_ref, st_ref, z0, z1, o_scr, st_scr, *, tq, nt, nblk, layer):
    s = pl.program_id(0)
    t_b = jnp.clip(s - 1, 0, nblk - 1) % nt

    @pl.when(s == 0)
    def _():
        z1[...] = jnp.zeros_like(z1)

    @pl.when(t_b == 0)
    def _():
        st_scr[...] = jnp.zeros_like(st_scr)

    def step(zw, zr):
        a_pieces = _hgrn_project_pieces(lambda: xa_ref[...], g_ref, win_ref, zw)
        slots = _hgrn_core_slots(zr, st_scr, o_scr, lbraw_ref, onorm_ref, seg_ref,
                                 nb=1, tq=tq, layer=layer)

        def out_proj():
            out_ref[...] = xb_ref[...] + _dot(o_scr[...], wo_ref[...])

        _interleave(slots, _pipeline_fillers(a_pieces, [], [], [out_proj], len(slots)))

    @pl.when(s % 2 == 0)
    def _():
        step(z0, z1)

    @pl.when(s % 2 == 1)
    def _():
        step(z1, z0)

    @pl.when((t_b == nt - 1) & (s >= 1))
    def _():
        st_ref[...] = st_scr[...]


def _piped_hgrn_layer(x, tq, g, w_in, lb_raw, onorm, w_o, seg, *, layer):
    b, t_len, _ = x.shape
    nt = t_len // tq
    nblk = b * nt
    x2 = x.reshape(b * t_len, D_MODEL)

    def blk_b(s):
        return jnp.clip(s - 1, 0, nblk - 1)

    y, st = pl.pallas_call(
        functools.partial(_piped_hgrn_kernel, tq=tq, nt=nt, nblk=nblk, layer=layer),
        grid=(nblk + 1,),
        in_specs=[pl.BlockSpec((tq, D_MODEL), lambda s: (jnp.minimum(s, nblk - 1), 0)),
                  pl.BlockSpec((tq, D_MODEL), lambda s: (blk_b(s), 0)),
                  _resident((1, D_MODEL)),
                  _layer_weight((D_MODEL, HG_IN), layer),
                  _resident((lb_raw.shape[0], HG_F)), _resident((1, HG_EXPAND)),
                  _layer_weight((D_MODEL, D_MODEL), layer),
                  _resident((CHUNK, 3 * CHUNK))],
        out_specs=[pl.BlockSpec((tq, D_MODEL), lambda s: (blk_b(s), 0)),
                   pl.BlockSpec((1, HG_HEADS, HG_EXPAND, HG_EXPAND),
                                lambda s: (blk_b(s) // nt, 0, 0, 0))],
        out_shape=[jax.ShapeDtypeStruct((b * t_len, D_MODEL), F32),
                   jax.ShapeDtypeStruct((b, HG_HEADS, HG_EXPAND, HG_EXPAND), F32)],
        scratch_shapes=[pltpu.VMEM((tq, HG_IN), F32), pltpu.VMEM((tq, HG_IN), F32),
                        pltpu.VMEM((tq, D_MODEL), BF16),
                        pltpu.VMEM((1, HG_HEADS, HG_EXPAND, HG_EXPAND), F32)],
        compiler_params=pltpu.CompilerParams(dimension_semantics=("arbitrary",),
                                             vmem_limit_bytes=VMEM_LIMIT_BYTES),
        name="hgrn_piped_layer",
    )(x2, x2, g, w_in, lb_raw, onorm, w_o, seg)
    return y.reshape(b, t_len, D_MODEL), st


def _fused_swa_layer(x, tq, mixer_args, mlp_args, *, layer, mlp_layer, final):
    b, t_len, _ = x.shape
    nt = t_len // tq
    nblk = b * nt
    x2 = x.reshape(b * t_len, D_MODEL)

    def blk_a(s):
        return jnp.minimum(s, nblk - 1)

    g, w_qkv, w_o, sinks, (cos, sa, sb) = mixer_args
    row_blk = lambda f: pl.BlockSpec((tq, D_MODEL), lambda s: (f(s), 0))
    tab = pl.BlockSpec((tq, LANES), lambda s: (blk_a(s) % nt, 0))
    cache_blk = pl.BlockSpec((1, WINDOW, KV_DIM), lambda s: (blk_a(s) // nt, 0, 0))
    cache_shape = jax.ShapeDtypeStruct((b, WINDOW, KV_DIM), F32)
    y, nk, nv = pl.pallas_call(
        functools.partial(_fused_swa_kernel, tq=tq, nt=nt, nblk=nblk, final=final),
        grid=(nblk + 2,),
        in_specs=[pl.BlockSpec(memory_space=pltpu.SMEM), row_blk(blk_a),
                  row_blk(lambda s: jnp.clip(s - 1, 0, nblk - 1)),
                  _resident((1, D_MODEL)),
                  _layer_weight((D_MODEL, Q_DIM + 2 * KV_DIM), layer),
                  _layer_weight((Q_DIM, D_MODEL), layer), tab, tab, tab,
                  _resident((1, D_MODEL)), _layer_weight((D_MODEL, D_FF), mlp_layer),
                  _layer_weight((D_FF, D_MODEL), mlp_layer), _resident((1, D_MODEL))],
        out_specs=[row_blk(lambda s: jnp.clip(s - 2, 0, nblk - 1)), cache_blk, cache_blk],
        out_shape=[jax.ShapeDtypeStruct((b * t_len, D_MODEL), F32), cache_shape, cache_shape],
        scratch_shapes=(_swa_scratch(1, tq) + _swa_scratch(1, tq)
                        + [pltpu.VMEM((tq, D_MODEL), F32), pltpu.VMEM((tq, D_MODEL), F32),
                           pltpu.VMEM((tq, D_MODEL), BF16)]
                        + [pltpu.VMEM((1, WINDOW, KV_DIM), F32) for _ in range(2)]),
        compiler_params=pltpu.CompilerParams(dimension_semantics=("arbitrary",),
                                             vmem_limit_bytes=VMEM_LIMIT_BYTES),
        name="swa_mlp_layer",
    )(sinks, x2, x2, g, w_qkv, w_o, cos, sa, sb, *mlp_args)
    return y.reshape(b, t_len, D_MODEL), nk, nv


def kernel(x_prompt, x_sample, cache_k, cache_v, state_s, mixer_norm, mlp_norm, attn_w_qkv,
           attn_w_o, attn_sinks, hgrn_w_in, hgrn_lb, hgrn_out_norm, hgrn_w_o, mlp_w_up,
           mlp_w_down, final_norm):
    bp, tp, _ = x_prompt.shape
    bs, ts, _ = x_sample.shape
    assert cache_k.shape[2] == WINDOW and ts % CHUNK == 0 and tp % CHUNK == 0

    w_qkv = attn_w_qkv.astype(BF16)
    w_ao = attn_w_o.astype(BF16)
    w_in = hgrn_w_in.astype(BF16)
    w_ho = hgrn_w_o.astype(BF16)
    w_up = mlp_w_up.astype(BF16)
    w_down = mlp_w_down.astype(BF16)
    lb_raw = hgrn_lb.astype(F32)
    seg = jnp.asarray(_segment_matrix(), BF16)
    final_g = final_norm.reshape(1, D_MODEL)

    rope_p = _rope_tables(jnp.arange(tp, dtype=F32))
    rope_s = _rope_tables(PAST_LEN + jnp.arange(ts, dtype=F32))
    tq_p = _row_block(tp, 256)
    tq_h = _row_block(tp, 512)
    assert tq_p >= WINDOW

    yp, ys = x_prompt, x_sample
    kp_l, vp_l, sp_l, ks_l, vs_l, ss_l = [], [], [], [], [], []
    for i in range(DEPTH):
        j = i // 2
        g = mixer_norm[i].reshape(1, D_MODEL)
        gm = mlp_norm[i].reshape(1, D_MODEL)
        final = i == DEPTH - 1
        mlp_args = (gm, w_up, w_down, final_g)
        if i % 2 == 0:
            yp, kp, vp = _fused_swa_layer(yp, tq_p, (g, w_qkv, w_ao, attn_sinks[j], rope_p),
                                          mlp_args, layer=j, mlp_layer=i, final=final)
            ys, kn, vn = _attn_layer(ys, g, w_qkv, w_ao, attn_sinks[j], rope_s,
                                     cache_k[j].reshape(bs, WINDOW, KV_DIM),
                                     cache_v[j].reshape(bs, WINDOW, KV_DIM),
                                     nb=bs, tq=ts, pos0=PAST_LEN, layer=j)
            kp_l.append(kp.reshape(bp, WINDOW, N_KV, HEAD_DIM))
            vp_l.append(vp.reshape(bp, WINDOW, N_KV, HEAD_DIM))
            ks_l.append(kn.reshape(bs, WINDOW, N_KV, HEAD_DIM))
            vs_l.append(vn.reshape(bs, WINDOW, N_KV, HEAD_DIM))
        else:
            onorm = hgrn_out_norm[j].reshape(1, HG_EXPAND)
            yp, sp = _piped_hgrn_layer(yp, tq_h, g, w_in, lb_raw, onorm, w_ho, seg, layer=j)
            yp = _mlp(yp.reshape(bp * tp, D_MODEL), gm, w_up, w_down, final_g,
                      final, i).reshape(bp, tp, D_MODEL)
            ys, sn = _hgrn_layer(ys, g, w_in, lb_raw, onorm, w_ho, seg, state_s[j],
                                 nb=bs, tq=ts, layer=j)
            sp_l.append(sp)
            ss_l.append(sn)
        ys = _mlp(ys.reshape(bs * ts, D_MODEL), gm, w_up, w_down, final_g,
                  final, i).reshape(bs, ts, D_MODEL)
    return (yp, ys, jnp.stack(kp_l), jnp.stack(vp_l), jnp.stack(sp_l),
            jnp.stack(ks_l), jnp.stack(vs_l), jnp.stack(ss_l))
```
